```python
import jax, jax.numpy as jnp
from jax import lax
import numpy as np

D_MODEL = 1024
BATCH = 2
SEQ = 16384
DEPTH = 4

GRID_W = 64
CTX_LEN = 256
NORM_EPS = 1e-6
LN_EPS = 1e-5

MLA_HEADS = 8
MLA_NOPE = 128
MLA_ROPE = 64
MLA_V = 128
Q_LORA = 512
KV_LORA = 256
ROPE_FREQS = MLA_ROPE // 4
ROPE_THETA = 10000.0
Q_BLOCK = 128

CONV_CH = D_MODEL
CONV_W = 31

RWKV_HEAD = 64
RWKV_HEADS = D_MODEL // RWKV_HEAD
RWKV_CH = RWKV_HEADS * RWKV_HEAD
DECAY_LORA = 64
ICL_LORA = 64
GATE_LORA = 128
GN_EPS = 64e-5
RWKV_IN = 3 * RWKV_CH + 2 * DECAY_LORA + 2 * ICL_LORA + GATE_LORA
RWKV_SPLITS = [RWKV_CH, 2 * RWKV_CH, 3 * RWKV_CH, 3 * RWKV_CH + 2 * DECAY_LORA,
               3 * RWKV_CH + 2 * DECAY_LORA + 2 * ICL_LORA]

D_FF = 4 * D_MODEL
N_BRANCH = 3

OFF_Q = Q_LORA
OFF_KV = OFF_Q + KV_LORA + MLA_ROPE
OFF_GLU = OFF_KV + 2 * CONV_CH
OFF_RWKV = OFF_GLU + RWKV_IN
D_IN = OFF_RWKV + N_BRANCH * D_MODEL

kernel_name = "hybrid_mla_conformer_rwkv7_dit"


def rmsnorm(x, g):
    xf = x.astype(jnp.float32)
    y = xf * lax.rsqrt(jnp.mean(xf * xf, axis=-1, keepdims=True) + NORM_EPS)
    return (y * g.astype(jnp.float32)).astype(x.dtype)


def layernorm(x, g, b):
    xf = x.astype(jnp.float32)
    mu = jnp.mean(xf, axis=-1, keepdims=True)
    var = jnp.mean(jnp.square(xf - mu), axis=-1, keepdims=True)
    y = (xf - mu) * lax.rsqrt(var + LN_EPS)
    return (y * g.astype(jnp.float32) + b.astype(jnp.float32)).astype(x.dtype)


def head_group_norm(y, g, b, dtype):
    mu = jnp.mean(y, axis=-1, keepdims=True)
    var = jnp.mean(jnp.square(y - mu), axis=-1, keepdims=True)
    out = ((y - mu) * lax.rsqrt(var + GN_EPS)).reshape(y.shape[:2] + (RWKV_CH,))
    return (out * g.astype(jnp.float32) + b.astype(jnp.float32)).astype(dtype)


def adaln(z, g, shift, scale):
    return rmsnorm(z, g) * (1.0 + scale) + shift


def apply_axial_rope(x, cos, sin):
    xs = x.reshape(x.shape[:-1] + (2, 2, ROPE_FREQS))
    x1, x2 = xs[..., 0, :], xs[..., 1, :]
    out = jnp.stack([x1 * cos - x2 * sin, x2 * cos + x1 * sin], axis=-2)
    return out.reshape(x.shape)


def mla_branch(q_a, kv_a, n_ctx, cos, sin, q_a_norm_g, w_q_b, kv_a_norm_g, w_kv_b, w_o):
    B, L, _ = q_a.shape
    q = (rmsnorm(q_a, q_a_norm_g) @ w_q_b).reshape(B, L, MLA_HEADS, MLA_NOPE + MLA_ROPE)
    q_nope = q[..., :MLA_NOPE]
    q_rope = apply_axial_rope(q[..., MLA_NOPE:], cos[None, :, None], sin[None, :, None])
    k_rope = apply_axial_rope(kv_a[..., KV_LORA:], cos[None], sin[None])
    kv = (rmsnorm(kv_a[..., :KV_LORA], kv_a_norm_g) @ w_kv_b).reshape(B, L, MLA_HEADS, MLA_NOPE + MLA_V)
    k_nope, v = kv[..., :MLA_NOPE], kv[..., MLA_NOPE:]
    scale = (MLA_NOPE + MLA_ROPE) ** -0.5

    def attend(qn, qr, kn, kr, vv):
        s = jnp.einsum("bqhd,bkhd->bhqk", qn, kn) + jnp.einsum("bqhr,bkr->bhqk", qr, kr)
        p = jax.nn.softmax(s.astype(jnp.float32) * scale, axis=-1).astype(vv.dtype)
        return jnp.einsum("bhqk,bkhd->bqhd", p, vv)

    o_ctx = attend(q_nope[:, :n_ctx], q_rope[:, :n_ctx], k_nope[:, :n_ctx], k_rope[:, :n_ctx], v[:, :n_ctx])
    n_lat = L - n_ctx
    nb = n_lat // Q_BLOCK

    def to_blocks(t):
        return jnp.moveaxis(t[:, n_ctx:].reshape((B, nb, Q_BLOCK) + t.shape[2:]), 1, 0)

    o_lat = lax.map(lambda qb: attend(qb[0], qb[1], k_nope, k_rope, v), (to_blocks(q_nope), to_blocks(q_rope)))
    o_lat = jnp.moveaxis(o_lat, 0, 1).reshape(B, n_lat, MLA_HEADS, MLA_V)
    o = jnp.concatenate([o_ctx, o_lat], axis=1).reshape(B, L, MLA_HEADS * MLA_V)
    return o @ w_o


def depthwise_conv(y, k, b):
    out = lax.conv_general_dilated(y, k[:, None, :].astype(y.dtype), window_strides=(1,),
                                   padding=[(CONV_W // 2, CONV_W // 2)],
                                   dimension_numbers=("NWC", "WIO", "NWC"),
                                   feature_group_count=y.shape[-1])
    return out + b


def conformer_branch(u, n_ctx, dw_kernel, dw_bias, ln_g, ln_b, w_pw2):
    a, gt = jnp.split(u, 2, axis=-1)
    y = a * jax.nn.sigmoid(gt)
    y = jnp.concatenate([depthwise_conv(y[:, :n_ctx], dw_kernel, dw_bias),
                         depthwise_conv(y[:, n_ctx:], dw_kernel, dw_bias)], axis=1)
    y = jax.nn.silu(layernorm(y, ln_g, ln_b))
    return y @ w_pw2


def centred_shift(p, mu):
    prev = jnp.pad(p, ((0, 0), (1, 0), (0, 0)))[:, :-1]
    nxt = jnp.pad(p, ((0, 0), (0, 1), (0, 0)))[:, 1:]
    return p + mu[0] * (prev - p) + mu[1] * (nxt - p)


def wkv_scan(r, w, k, v, kk, a, s0, reverse):
    xs = tuple(jnp.moveaxis(t.astype(jnp.float32), 1, 0) for t in (r, w, k, v, kk, a))

    def step(S, inp):
        r_t, w_t, k_t, v_t, kk_t, a_t = inp
        s_kk = jnp.einsum("bhvk,bhk->bhv", S, kk_t)
        S = (S * w_t[:, :, None, :] - s_kk[..., None] * (kk_t * a_t)[:, :, None, :]
             + v_t[..., None] * k_t[:, :, None, :])
        return S, jnp.einsum("bhvk,bhk->bhv", S, r_t)

    s_fin, ys = lax.scan(step, s0, xs, reverse=reverse)
    return s_fin, jnp.moveaxis(ys, 0, 1)


def rwkv_branch(u, n_ctx, mu, w0, w2, a0, a2, g2, k_k, k_a, r_k, ln_g, ln_b, w_o):
    B, L, _ = u.shape
    u = jnp.concatenate([centred_shift(u[:, :n_ctx], mu), centred_shift(u[:, n_ctx:], mu)], axis=1)
    r, k, v, w1, a1, g1 = jnp.split(u, RWKV_SPLITS, axis=-1)
    w1 = w1.reshape(B, L, 2, DECAY_LORA)
    a1 = a1.reshape(B, L, 2, ICL_LORA)
    w_log = -jax.nn.softplus(-(w0 + jnp.einsum("bldr,drc->bldc", jnp.tanh(w1), w2))) - 0.5
    decay = jnp.exp(-jnp.exp(w_log.astype(jnp.float32)))
    a = jax.nn.sigmoid(a0 + jnp.einsum("bldr,drc->bldc", a1, a2))
    g = jax.nn.sigmoid(g1) @ g2

    def hd(t):
        return t.reshape(t.shape[:-1] + (RWKV_HEADS, RWKV_HEAD))

    kk_f = hd(k * k_k).astype(jnp.float32)
    kk = kk_f * lax.rsqrt(jnp.maximum(jnp.sum(kk_f * kk_f, axis=-1, keepdims=True), 1e-24))
    k_dir = hd(k[:, :, None] * (1.0 + (a - 1.0) * k_a))
    a, decay, r, v = hd(a), hd(decay), hd(r), hd(v)

    y = jnp.zeros((B, L, RWKV_HEADS, RWKV_HEAD), jnp.float32)
    for d, rev in ((0, False), (1, True)):
        seq_in = (r, decay[:, :, d], k_dir[:, :, d], v, kk, a[:, :, d])
        s0 = jnp.zeros((B, RWKV_HEADS, RWKV_HEAD, RWKV_HEAD), jnp.float32)
        ctx_in = [t[:, :n_ctx] for t in seq_in]
        lat_in = [t[:, n_ctx:] for t in seq_in]
        s_ctx, y_ctx = wkv_scan(*ctx_in, s0, rev)
        _, y_lat = wkv_scan(*lat_in, s_ctx, rev)
        y = y + jnp.concatenate([y_ctx, y_lat], axis=1)

    y = head_group_norm(y, ln_g, ln_b, u.dtype)
    bonus = jnp.einsum("blhn,bldhn,hn->blh", r, k_dir, r_k)[..., None] * v
    y = (y + bonus.reshape(B, L, RWKV_CH)) * g
    return y @ w_o


def squared_relu_mlp(h, w_up, w_down):
    return jnp.square(jax.nn.relu(h @ w_up)) @ w_down


def setup_inputs(seed: int = 0) -> dict:
    key = jax.random.key(seed)
    keys = iter(jax.random.split(key, 64))

    def nrm(shape, std):
        return jax.random.normal(next(keys), shape, jnp.float32) * std

    def gain(shape):
        return 1.0 + nrm(shape, 0.02)

    L, D, C, H, N = DEPTH, D_MODEL, RWKV_CH, RWKV_HEADS, RWKV_HEAD
    decay_base = jnp.linspace(-6.5, -1.5, C, dtype=jnp.float32)
    return {
        "x": nrm((BATCH, SEQ, D), 1.0),
        "c": nrm((BATCH, D), 1.0),
        "ctx": nrm((BATCH, CTX_LEN, D), 1.0),
        "c_ctx": nrm((D,), 1.0),
        "w_mod": nrm((L, D, 6 * D), 0.5 * D ** -0.5),
        "b_mod": nrm((L, 6 * D), 0.01),
        "norm1_g": gain((L, D)),
        "norm2_g": gain((L, D)),
        "w_in": nrm((L, D, D_IN), D ** -0.5),
        "q_a_norm_g": gain((L, Q_LORA)),
        "w_q_b": nrm((L, Q_LORA, MLA_HEADS * (MLA_NOPE + MLA_ROPE)), Q_LORA ** -0.5),
        "kv_a_norm_g": gain((L, KV_LORA)),
        "w_kv_b": nrm((L, KV_LORA, MLA_HEADS * (MLA_NOPE + MLA_V)), KV_LORA ** -0.5),
        "w_o_mla": nrm((L, MLA_HEADS * MLA_V, D), (MLA_HEADS * MLA_V) ** -0.5),
        "dw_kernel": nrm((L, CONV_W, CONV_CH), CONV_W ** -0.5),
        "dw_bias": nrm((L, CONV_CH), 0.01),
        "conv_ln_g": gain((L, CONV_CH)),
        "conv_ln_b": nrm((L, CONV_CH), 0.01),
        "w_pw2": nrm((L, CONV_CH, D), CONV_CH ** -0.5),
        "rwkv_mu": jax.random.uniform(next(keys), (L, 2, RWKV_IN), jnp.float32, 0.0, 0.5),
        "rwkv_w0": decay_base + nrm((L, 2, C), 0.1),
        "rwkv_w2": nrm((L, 2, DECAY_LORA, C), 0.1 * DECAY_LORA ** -0.5),
        "rwkv_a0": nrm((L, 2, C), 0.1),
        "rwkv_a2": nrm((L, 2, ICL_LORA, C), 0.5 * ICL_LORA ** -0.5),
        "rwkv_g2": nrm((L, GATE_LORA, C), GATE_LORA ** -0.5),
        "rwkv_k_k": 0.85 + nrm((L, C), 0.02),
        "rwkv_k_a": 1.0 + nrm((L, C), 0.02),
        "rwkv_r_k": nrm((L, H, N), 0.1),
        "rwkv_ln_g": gain((L, C)),
        "rwkv_ln_b": nrm((L, C), 0.01),
        "w_o_rwkv": nrm((L, C, D), C ** -0.5),
        "w_out": nrm((L, D, D), D ** -0.5),
        "w_up": nrm((L, D, D_FF), D ** -0.5),
        "w_down": nrm((L, D_FF, D), D_FF ** -0.5),
        "final_norm_g": gain((D,)),
    }


def reference(x, c, ctx, c_ctx, w_mod, b_mod, norm1_g, norm2_g, w_in, q_a_norm_g, w_q_b, kv_a_norm_g,
              w_kv_b, w_o_mla, dw_kernel, dw_bias, conv_ln_g, conv_ln_b, w_pw2, rwkv_mu, rwkv_w0, rwkv_w2,
              rwkv_a0, rwkv_a2, rwkv_g2, rwkv_k_k, rwkv_k_a, rwkv_r_k, rwkv_ln_g, rwkv_ln_b, w_o_rwkv, w_out,
              w_up, w_down, final_norm_g):
    B, n_lat, _ = x.shape
    n_ctx = ctx.shape[1]
    ROWS = n_lat // GRID_W
    row = jnp.repeat(jnp.arange(ROWS), GRID_W).astype(jnp.float32)
    col = jnp.tile(jnp.arange(GRID_W), ROWS).astype(jnp.float32)
    inv_freq = ROPE_THETA ** (-jnp.arange(ROPE_FREQS, dtype=jnp.float32) / ROPE_FREQS)
    ang_lat = jnp.stack([row[:, None] * inv_freq, col[:, None] * inv_freq], axis=1)
    ang = jnp.concatenate([jnp.zeros((n_ctx, 2, ROPE_FREQS), jnp.float32), ang_lat], axis=0)
    cos, sin = jnp.cos(ang).astype(x.dtype), jnp.sin(ang).astype(x.dtype)

    sc_lat = jax.nn.silu(c)
    sc_ctx = jax.nn.silu(c_ctx)
    for l in range(DEPTH):
        last = l == DEPTH - 1
        m_lat = (sc_lat @ w_mod[l] + b_mod[l]).reshape(B, 6, 1, D_MODEL)
        m_ctx = (sc_ctx @ w_mod[l] + b_mod[l]).reshape(6, D_MODEL)

        h = jnp.concatenate([adaln(ctx, norm1_g[l], m_ctx[0], m_ctx[1]),
                             adaln(x, norm1_g[l], m_lat[:, 0], m_lat[:, 1])], axis=1)
        p = h @ w_in[l]
        q_a, kv_a, glu_in, rw_in, gate_in = jnp.split(p, [OFF_Q, OFF_KV, OFF_GLU, OFF_RWKV], axis=-1)
        y_a = mla_branch(q_a, kv_a, n_ctx, cos, sin, q_a_norm_g[l], w_q_b[l], kv_a_norm_g[l], w_kv_b[l],
                         w_o_mla[l])
        y_b = conformer_branch(glu_in, n_ctx, dw_kernel[l], dw_bias[l], conv_ln_g[l], conv_ln_b[l], w_pw2[l])
        y_c = rwkv_branch(rw_in, n_ctx, rwkv_mu[l], rwkv_w0[l], rwkv_w2[l], rwkv_a0[l], rwkv_a2[l], rwkv_g2[l],
                          rwkv_k_k[l], rwkv_k_a[l], rwkv_r_k[l], rwkv_ln_g[l], rwkv_ln_b[l], w_o_rwkv[l])
        gates = jax.nn.sigmoid(gate_in.reshape(gate_in.shape[:2] + (N_BRANCH, D_MODEL)))
        y = (gates[:, :, 0] * y_a + gates[:, :, 1] * y_b + gates[:, :, 2] * y_c) @ w_out[l]
        x = x + m_lat[:, 2] * y[:, n_ctx:]
        if not last:
            ctx = ctx + m_ctx[2] * y[:, :n_ctx]

        x = x + m_lat[:, 5] * squared_relu_mlp(adaln(x, norm2_g[l], m_lat[:, 3], m_lat[:, 4]), w_up[l], w_down[l])
        if not last:
            ctx = ctx + m_ctx[5] * squared_relu_mlp(adaln(ctx, norm2_g[l], m_ctx[3], m_ctx[4]), w_up[l], w_down[l])

    return rmsnorm(x, final_norm_g)
```

```python
import functools
import math

import jax
import jax.numpy as jnp
from jax import lax
from jax.experimental import pallas as pl
from jax.experimental.pallas import tpu as pltpu

F32 = jnp.float32
BF16 = jnp.bfloat16
HIGHEST = lax.Precision.HIGHEST

D_MODEL = 1024
GRID_W = 64
NORM_EPS = 1e-6
LN_EPS = 1e-5
MLA_HEADS = 8
MLA_NOPE = 128
MLA_ROPE = 64
MLA_V = 128
Q_LORA = 512
KV_LORA = 256
ROPE_FREQS = MLA_ROPE // 4
ROPE_THETA = 10000.0
CONV_W = 31
RWKV_HEAD = 64
RWKV_HEADS = D_MODEL // RWKV_HEAD
DECAY_LORA = 64
ICL_LORA = 64
GATE_LORA = 128
GN_EPS = 64e-5
D_FF = 4 * D_MODEL

LANES = 128
BF16_SUBLANES = 16
VMEM_LIMIT = 56 * 1024 * 1024

COL_GATES = 0
COL_GLU_A = 3 * D_MODEL
COL_GLU_G = 4 * D_MODEL
COL_R = 5 * D_MODEL
COL_K = 6 * D_MODEL
COL_V = 7 * D_MODEL
COL_QA = 8 * D_MODEL
COL_KVL = COL_QA + Q_LORA
COL_KR = COL_KVL + KV_LORA
COL_LORA = COL_KR + 2 * LANES
D_IN_PAD = 19 * 512
HEAD_QK = 2 * LANES

SCAN_CHUNK = 64
PAIR = 2 * RWKV_HEAD


def _cparams(sem):
    return pltpu.CompilerParams(dimension_semantics=sem, vmem_limit_bytes=VMEM_LIMIT)


def _pick_tile(n, prefs):
    for t in prefs:
        if n % t == 0:
            return t
    raise ValueError(f"no tile in {prefs} divides {n}")


def _sigmoid(x):
    return 1.0 / (1.0 + jnp.exp(-x))


def _mod_kernel(x_ref, w_ref, b_ref, o_ref):
    x = x_ref[...]
    s = x * _sigmoid(x)
    o_ref[0] = jnp.dot(s, w_ref[0], precision=HIGHEST, preferred_element_type=F32) + b_ref[0]


def _modulation(xc, w_mod, b_mod):
    depth, d, n = w_mod.shape
    tn = _pick_tile(n, (1536, 1024, 512, 128))
    return pl.pallas_call(
        _mod_kernel,
        grid=(depth, n // tn),
        in_specs=[
            pl.BlockSpec((8, d), lambda l, j: (0, 0)),
            pl.BlockSpec((1, d, tn), lambda l, j: (l, 0, j)),
            pl.BlockSpec((1, 1, tn), lambda l, j: (l, 0, j)),
        ],
        out_specs=pl.BlockSpec((1, 8, tn), lambda l, j: (l, 0, j)),
        out_shape=jax.ShapeDtypeStruct((depth, 8, n), F32),
        compiler_params=_cparams(("parallel", "parallel")),
        name="modulation",
    )(xc, w_mod, b_mod.reshape(depth, 1, n))


def _mod_rows(mod_ref, slot, batch, row0, tm, n_ctx, ctx_row):
    rows = row0 + lax.broadcasted_iota(jnp.int32, (tm, 1), 0)
    lat = mod_ref[slot, pl.ds(batch, 1), :]
    ctx = mod_ref[slot, pl.ds(ctx_row, 1), :]
    return jnp.where(rows < n_ctx, ctx, lat)


def _rms(x, g):
    return x * lax.rsqrt(jnp.mean(x * x, axis=-1, keepdims=True) + NORM_EPS) * g


def _adaln(z, g, mod_ref, s_shift, s_scale, batch, row0, tm, n_ctx, ctx_row):
    shift = _mod_rows(mod_ref, s_shift, batch, row0, tm, n_ctx, ctx_row)
    scale = _mod_rows(mod_ref, s_scale, batch, row0, tm, n_ctx, ctx_row)
    return _rms(z, g) * (1.0 + scale) + shift


def _in_proj_kernel(z_ref, mod_ref, g_ref, w_ref, o_ref, h_ref, *, tm, tiles, n_ctx, ctx_row):
    i = pl.program_id(0)

    @pl.when(pl.program_id(1) == 0)
    def _():
        h = _adaln(z_ref[...], g_ref[...], mod_ref, 0, 1, i // tiles, (i % tiles) * tm, tm, n_ctx, ctx_row)
        h_ref[...] = h.astype(BF16)

    o_ref[...] = jnp.dot(h_ref[...], w_ref[...], preferred_element_type=F32).astype(o_ref.dtype)


def _in_proj(z, mod, g, w, *, seq, n_ctx, ctx_row):
    m, d = z.shape
    n = w.shape[1]
    tm = _pick_tile(seq, (1280, 640, 320, 256))
    tn = 512
    kern = functools.partial(_in_proj_kernel, tm=tm, tiles=seq // tm, n_ctx=n_ctx, ctx_row=ctx_row)
    return pl.pallas_call(
        kern,
        grid=(m // tm, n // tn),
        in_specs=[
            pl.BlockSpec((tm, d), lambda i, j: (i, 0)),
            pl.BlockSpec(mod.shape, lambda i, j: (0, 0, 0)),
            pl.BlockSpec((1, d), lambda i, j: (0, 0)),
            pl.BlockSpec((d, tn), lambda i, j: (0, j)),
        ],
        out_specs=pl.BlockSpec((tm, tn), lambda i, j: (i, j)),
        out_shape=jax.ShapeDtypeStruct((m, n), BF16),
        scratch_shapes=[pltpu.VMEM((tm, d), BF16)],
        compiler_params=_cparams(("parallel", "arbitrary")),
        name="in_proj",
    )(z, mod, g, w)


def _qkv_kernel(qa_ref, kvl_ref, kr_ref, cos_ref, sin_ref, qg_ref, kvg_ref, wq_ref, wkv_ref,
                q_ref, k_ref, v_ref, *, q_scale):
    cos = cos_ref[...]
    sin = sin_ref[...]
    qn = _rms(qa_ref[...].astype(F32), qg_ref[...]).astype(BF16)
    kvn = _rms(kvl_ref[...].astype(F32), kvg_ref[...]).astype(BF16)
    kr = kr_ref[...].astype(F32)
    k_rope = (kr[:, :LANES] * cos + kr[:, LANES:] * sin).astype(BF16)
    for h in range(MLA_HEADS):
        qh = jnp.dot(qn, wq_ref[:, h * 3 * LANES:(h + 1) * 3 * LANES], preferred_element_type=F32)
        rope = qh[:, LANES:2 * LANES] * cos + qh[:, 2 * LANES:] * sin
        q_ref[0, h, :, :LANES] = (qh[:, :LANES] * q_scale).astype(BF16)
        q_ref[0, h, :, LANES:] = (rope * q_scale).astype(BF16)
        kvh = jnp.dot(kvn, wkv_ref[:, h * 2 * LANES:(h + 1) * 2 * LANES], preferred_element_type=F32)
        k_ref[0, h, :, :LANES] = kvh[:, :LANES].astype(BF16)
        k_ref[0, h, :, LANES:] = k_rope
        v_ref[0, h] = kvh[:, LANES:].astype(BF16)


def _qkv(p, cos_t, sin_t, qg, kvg, wq, wkv, *, batch, seq):
    tm = _pick_tile(seq, (640, 320, 256))
    tiles = seq // tm
    q_scale = (MLA_NOPE + MLA_ROPE) ** -0.5 * math.log2(math.e)
    kern = functools.partial(_qkv_kernel, q_scale=q_scale)
    row = lambda b, t: b * tiles + t
    return pl.pallas_call(
        kern,
        grid=(batch, tiles),
        in_specs=[
            pl.BlockSpec((tm, Q_LORA), lambda b, t: (row(b, t), COL_QA // Q_LORA)),
            pl.BlockSpec((tm, KV_LORA), lambda b, t: (row(b, t), COL_KVL // KV_LORA)),
            pl.BlockSpec((tm, 2 * LANES), lambda b, t: (row(b, t), COL_KR // (2 * LANES))),
            pl.BlockSpec((tm, LANES), lambda b, t: (t, 0)),
            pl.BlockSpec((tm, LANES), lambda b, t: (t, 0)),
            pl.BlockSpec((1, Q_LORA), lambda b, t: (0, 0)),
            pl.BlockSpec((1, KV_LORA), lambda b, t: (0, 0)),
            pl.BlockSpec(wq.shape, lambda b, t: (0, 0)),
            pl.BlockSpec(wkv.shape, lambda b, t: (0, 0)),
        ],
        out_specs=[
            pl.BlockSpec((1, MLA_HEADS, tm, HEAD_QK), lambda b, t: (b, 0, t, 0)),
            pl.BlockSpec((1, MLA_HEADS, tm, HEAD_QK), lambda b, t: (b, 0, t, 0)),
            pl.BlockSpec((1, MLA_HEADS, tm, MLA_V), lambda b, t: (b, 0, t, 0)),
        ],
        out_shape=[
            jax.ShapeDtypeStruct((batch, MLA_HEADS, seq, HEAD_QK), BF16),
            jax.ShapeDtypeStruct((batch, MLA_HEADS, seq, HEAD_QK), BF16),
            jax.ShapeDtypeStruct((batch, MLA_HEADS, seq, MLA_V), BF16),
        ],
        compiler_params=_cparams(("parallel", "parallel")),
        name="qkv",
    )(p, p, p, cos_t, sin_t, qg, kvg, wq, wkv)


def _attn_kernel(q_ref, k_ref, v_ref, o_ref, *, tq, tk, seq, n_ctx):
    qi = pl.program_id(2)
    q = q_ref[0, 0]

    def step(kc, vc, carry):
        m, l, acc = carry
        s = lax.dot_general(q, kc, (((1,), (1,)), ((), ())), preferred_element_type=F32)
        m_new = jnp.maximum(m, jnp.max(s, axis=-1, keepdims=True))
        alpha = jnp.exp2(m - m_new)
        p = jnp.exp2(s - m_new)
        l = alpha * l + jnp.sum(p, axis=-1, keepdims=True)
        acc = alpha * acc + jnp.dot(p.astype(BF16), vc, preferred_element_type=F32)
        return m_new, l, acc

    init = (jnp.full((tq, 1), -1e30, F32), jnp.zeros((tq, 1), F32), jnp.zeros((tq, MLA_V), F32))

    def finish(carry):
        _, l, acc = carry
        o_ref[0] = (acc / l).astype(o_ref.dtype)

    @pl.when(qi < n_ctx // tq)
    def _():
        finish(step(k_ref[0, 0, :n_ctx, :], v_ref[0, 0, :n_ctx, :], init))

    @pl.when(qi >= n_ctx // tq)
    def _():
        def body(c, carry):
            start = pl.multiple_of(c * tk, tk)
            return step(k_ref[0, 0, pl.ds(start, tk), :], v_ref[0, 0, pl.ds(start, tk), :], carry)

        finish(lax.fori_loop(0, seq // tk, body, init))


def _attention(q, k, v, *, n_ctx):
    batch, heads, seq, _ = q.shape
    tq = _pick_tile(n_ctx, (256, 128))
    tk = _pick_tile(seq, (1280, 640, 256))
    kern = functools.partial(_attn_kernel, tq=tq, tk=tk, seq=seq, n_ctx=n_ctx)
    return pl.pallas_call(
        kern,
        grid=(batch, heads, seq // tq),
        in_specs=[
            pl.BlockSpec((1, 1, tq, HEAD_QK), lambda b, h, i: (b, h, i, 0)),
            pl.BlockSpec((1, 1, seq, HEAD_QK), lambda b, h, i: (b, h, 0, 0)),
            pl.BlockSpec((1, 1, seq, MLA_V), lambda b, h, i: (b, h, 0, 0)),
        ],
        out_specs=pl.BlockSpec((1, tq, MLA_V), lambda b, h, i: (b, i, h)),
        out_shape=jax.ShapeDtypeStruct((batch, seq, heads * MLA_V), BF16),
        compiler_params=_cparams(("parallel", "parallel", "arbitrary")),
        name="attention",
    )(q, k, v)


HALO = BF16_SUBLANES


def _segment_edges(t, tm, tiles, n_ctx):
    start = t * tm
    has_prev = jnp.logical_and(start != 0, start != n_ctx)
    has_next = jnp.logical_and(start + tm != n_ctx, t != tiles - 1)
    return has_prev, has_next


def _conv_kernel(a_ref, g_ref, ap_ref, gp_ref, an_ref, gn_ref, kern_ref, bias_ref, lng_ref, lnb_ref,
                 o_ref, buf_ref, *, tm, tiles, n_ctx):
    t = pl.program_id(1)
    has_prev, has_next = _segment_edges(t, tm, tiles, n_ctx)

    def glu(a, g):
        return a[...].astype(F32) * _sigmoid(g[...].astype(F32))

    buf_ref[0:HALO, :] = jnp.where(has_prev, glu(ap_ref, gp_ref), 0.0)
    buf_ref[HALO:HALO + tm, :] = glu(a_ref, g_ref)
    buf_ref[HALO + tm:, :] = jnp.where(has_next, glu(an_ref, gn_ref), 0.0)

    half = CONV_W // 2
    rb = 32
    for r0 in range(0, tm, rb):
        acc = jnp.zeros((rb, D_MODEL), F32) + bias_ref[...]
        for j in range(CONV_W):
            acc = acc + buf_ref[r0 + HALO - half + j:r0 + HALO - half + j + rb, :] * kern_ref[j:j + 1, :]
        mu = jnp.mean(acc, axis=-1, keepdims=True)
        dlt = acc - mu
        var = jnp.mean(dlt * dlt, axis=-1, keepdims=True)
        y = dlt * lax.rsqrt(var + LN_EPS) * lng_ref[...] + lnb_ref[...]
        o_ref[r0:r0 + rb, :] = (y * _sigmoid(y)).astype(o_ref.dtype)


def _conv(p, kern, bias, lng, lnb, *, batch, seq, n_ctx):
    tm = _pick_tile(n_ctx, (256, 128))
    tiles = seq // tm
    per = tm // HALO
    last_halo = batch * seq // HALO - 1
    ca, cg = COL_GLU_A // D_MODEL, COL_GLU_G // D_MODEL
    cur = lambda c: (lambda b, t: (b * tiles + t, c))
    prev = lambda c: (lambda b, t: (jnp.maximum((b * tiles + t) * per - 1, 0), c))
    nxt = lambda c: (lambda b, t: (jnp.minimum((b * tiles + t + 1) * per, last_halo), c))
    full = lambda shape: pl.BlockSpec(shape, lambda b, t: (0, 0))
    kfn = functools.partial(_conv_kernel, tm=tm, tiles=tiles, n_ctx=n_ctx)
    return pl.pallas_call(
        kfn,
        grid=(batch, tiles),
        in_specs=[
            pl.BlockSpec((tm, D_MODEL), cur(ca)),
            pl.BlockSpec((tm, D_MODEL), cur(cg)),
            pl.BlockSpec((HALO, D_MODEL), prev(ca)),
            pl.BlockSpec((HALO, D_MODEL), prev(cg)),
            pl.BlockSpec((HALO, D_MODEL), nxt(ca)),
            pl.BlockSpec((HALO, D_MODEL), nxt(cg)),
            full(kern.shape), full(bias.shape), full(lng.shape), full(lnb.shape),
        ],
        out_specs=pl.BlockSpec((tm, D_MODEL), lambda b, t: (b * tiles + t, 0)),
        out_shape=jax.ShapeDtypeStruct((batch * seq, D_MODEL), BF16),
        scratch_shapes=[pltpu.VMEM((tm + 2 * HALO, D_MODEL), F32)],
        compiler_params=_cparams(("parallel", "parallel")),
        name="conv",
    )(p, p, p, p, p, p, kern, bias, lng, lnb)


def _head_sum_bcast(x, e_ref, et_ref):
    s = jnp.dot(x, e_ref[...], precision=HIGHEST, preferred_element_type=F32)
    return jnp.dot(s, et_ref[...], precision=HIGHEST, preferred_element_type=F32)


def _rwkv_prep_kernel(*refs, tm, tiles, n_ctx):
    (r_c, k_c, v_c, l_c, r_p, k_p, v_p, l_p, r_n, k_n, v_n, l_n,
     mu_rkv, mu_lora, w0_ref, a0_ref, w2_ref, a2_ref, g2_ref, kk_ref, ka_ref, rk_ref, e_ref, et_ref,
     r_o, v_o, kk_o, kd_o, b_o, w_o, g_o, bonus_o) = refs
    t = pl.program_id(1)
    has_prev, has_next = _segment_edges(t, tm, tiles, n_ctx)
    row = lax.broadcasted_iota(jnp.int32, (tm, 1), 0)

    def shifted(cur, prv, nxt, mu0, mu1):
        x = cur[...].astype(F32)
        before = jnp.where(has_prev, prv[HALO - 1:HALO, :].astype(F32), 0.0)
        after = jnp.where(has_next, nxt[0:1, :].astype(F32), 0.0)
        x_prev = jnp.where(row == 0, before, pltpu.roll(x, 1, 0))
        x_next = jnp.where(row == tm - 1, after, pltpu.roll(x, tm - 1, 0))
        return x + mu0 * (x_prev - x) + mu1 * (x_next - x)

    def col(ref, i):
        return ref[:, i * D_MODEL:(i + 1) * D_MODEL]

    r = shifted(r_c, r_p, r_n, mu_rkv[0:1, 0:D_MODEL], mu_rkv[1:2, 0:D_MODEL])
    k = shifted(k_c, k_p, k_n, mu_rkv[0:1, D_MODEL:2 * D_MODEL], mu_rkv[1:2, D_MODEL:2 * D_MODEL])
    v = shifted(v_c, v_p, v_n, mu_rkv[0:1, 2 * D_MODEL:], mu_rkv[1:2, 2 * D_MODEL:])
    lo = shifted(l_c, l_p, l_n, mu_lora[0:1, :], mu_lora[1:2, :])
    w1, a1, g1 = lo[:, :LANES], lo[:, LANES:2 * LANES], lo[:, 2 * LANES:]

    lw = jnp.dot(jnp.tanh(w1).astype(BF16), w2_ref[...], preferred_element_type=F32)
    la = jnp.dot(a1.astype(BF16), a2_ref[...], preferred_element_type=F32)
    g = jnp.dot(_sigmoid(g1).astype(BF16), g2_ref[...], preferred_element_type=F32)

    kk = k * kk_ref[...]
    kk = kk * lax.rsqrt(jnp.maximum(_head_sum_bcast(kk * kk, e_ref, et_ref), 1e-24))
    kd_sum = jnp.zeros_like(k)
    for d in range(2):
        x = -(col(w0_ref, d) + col(lw, d))
        softplus = jnp.maximum(x, 0.0) + jnp.log(1.0 + jnp.exp(-jnp.abs(x)))
        w_o[d] = -jnp.exp(-softplus - 0.5)
        a = _sigmoid(col(a0_ref, d) + col(la, d))
        kd = k * (1.0 + (a - 1.0) * ka_ref[...])
        kd_sum = kd_sum + kd
        kd_o[d] = kd.astype(BF16)
        b_o[d] = (kk * a).astype(BF16)
    bonus = _head_sum_bcast(r * kd_sum * rk_ref[...], e_ref, et_ref) * v
    r_o[...] = r.astype(BF16)
    v_o[...] = v.astype(BF16)
    kk_o[...] = kk.astype(BF16)
    g_o[...] = g.astype(BF16)
    bonus_o[...] = bonus.astype(BF16)


def _rwkv_prep(p, mu_rkv, mu_lora, w0, a0, w2, a2, g2, k_k, k_a, r_k, e, et, *, batch, seq, n_ctx):
    tm = _pick_tile(n_ctx, (256, 128))
    tiles = seq // tm
    per = tm // HALO
    m = batch * seq
    last_halo = m // HALO - 1
    cur = lambda c: (lambda b, t: (b * tiles + t, c))
    prev = lambda c: (lambda b, t: (jnp.maximum((b * tiles + t) * per - 1, 0), c))
    nxt = lambda c: (lambda b, t: (jnp.minimum((b * tiles + t + 1) * per, last_halo), c))
    lora_w = 3 * LANES
    cols = [(D_MODEL, COL_R // D_MODEL), (D_MODEL, COL_K // D_MODEL), (D_MODEL, COL_V // D_MODEL),
            (lora_w, COL_LORA // lora_w)]
    in_specs = [pl.BlockSpec((tm, w), cur(c)) for w, c in cols]
    in_specs += [pl.BlockSpec((HALO, w), prev(c)) for w, c in cols]
    in_specs += [pl.BlockSpec((HALO, w), nxt(c)) for w, c in cols]
    consts = [mu_rkv, mu_lora, w0, a0, w2, a2, g2, k_k, k_a, r_k, e, et]
    in_specs += [pl.BlockSpec(c.shape, lambda b, t: (0, 0)) for c in consts]
    row_spec = pl.BlockSpec((tm, D_MODEL), lambda b, t: (b * tiles + t, 0))
    dir_spec = pl.BlockSpec((2, tm, D_MODEL), lambda b, t: (0, b * tiles + t, 0))
    tok = lambda dt: jax.ShapeDtypeStruct((m, D_MODEL), dt)
    dirs = lambda dt: jax.ShapeDtypeStruct((2, m, D_MODEL), dt)
    kfn = functools.partial(_rwkv_prep_kernel, tm=tm, tiles=tiles, n_ctx=n_ctx)
    return pl.pallas_call(
        kfn,
        grid=(batch, tiles),
        in_specs=in_specs,
        out_specs=[row_spec, row_spec, row_spec, dir_spec, dir_spec, dir_spec, row_spec, row_spec],
        out_shape=[tok(BF16), tok(BF16), tok(BF16), dirs(BF16), dirs(BF16), dirs(F32), tok(BF16), tok(BF16)],
        compiler_params=_cparams(("parallel", "parallel")),
        name="rwkv_prep",
    )(*([p] * 12), *consts)


def _nt(a, b):
    return lax.dot_general(a, b, (((1,), (1,)), ((), ())), preferred_element_type=F32)


def _tn(a, b):
    return lax.dot_general(a, b, (((0,), (0,)), ((), ())), preferred_element_type=F32)


def _mm(a, b):
    return jnp.dot(a.astype(BF16), b.astype(BF16), preferred_element_type=F32)


def _scan_kernel(r_ref, v_ref, kk_ref, kd_ref, b_ref, w_ref, y_ref, s_ref):
    c_len = SCAN_CHUNK
    d = pl.program_id(1)

    @pl.when(pl.program_id(2) == 0)
    def _():
        s_ref[...] = jnp.zeros_like(s_ref)

    sign = 1 - 2 * d
    ri = lax.broadcasted_iota(jnp.int32, (c_len, c_len), 0)
    ci = lax.broadcasted_iota(jnp.int32, (c_len, c_len), 1)
    tri = jnp.where((ri - ci) * sign >= 0, 1.0, 0.0).astype(F32)

    w = w_ref[0]
    cum = jnp.dot(tri, w, precision=HIGHEST, preferred_element_type=F32)
    tot = jnp.sum(w, axis=0, keepdims=True)

    r2 = lax.broadcasted_iota(jnp.int32, (PAIR, PAIR), 0)
    c2 = lax.broadcasted_iota(jnp.int32, (PAIR, PAIR), 1)
    same = (r2 >> 6) == (c2 >> 6)
    dist = ((r2 & (c_len - 1)) - (c2 & (c_len - 1))) * sign
    strict = jnp.logical_and(same, dist > 0)
    incl = jnp.logical_and(same, dist >= 0)
    eye = r2 == c2
    halves = [jnp.logical_and((r2 >> (l + 1)) == (c2 >> (l + 1)), ((r2 >> l) & 1) != ((c2 >> l) & 1))
              for l in range(6)]

    for pr in range(D_MODEL // PAIR):
        sl = slice(pr * PAIR, (pr + 1) * PAIR)

        def stack(x):
            return jnp.where(same, jnp.concatenate([x, x], axis=0), 0.0)

        cum_p, w_p, tot_p = cum[:, sl], w[:, sl], tot[:, sl]
        g_in = jnp.exp(cum_p)
        g_ex = jnp.exp(cum_p - w_p)
        g_inv = jnp.exp(-cum_p)
        g_tail = jnp.exp(tot_p - cum_p)
        r = r_ref[:, sl].astype(F32)
        v = v_ref[:, sl].astype(F32)
        kk = kk_ref[:, sl].astype(F32)
        kd = kd_ref[0, :, sl].astype(F32)
        b = b_ref[0, :, sl].astype(F32)

        rt = stack(r * g_in)
        kt = stack(kk * g_ex)
        vs = stack(v).astype(BF16)
        kdt = stack(kd * g_inv)
        bt = stack(b * g_inv)
        kh = stack(kd * g_tail).astype(BF16)
        bh = stack(b * g_tail).astype(BF16)

        lhs = jnp.concatenate([kt, rt], axis=0).astype(BF16)
        rhs = jnp.concatenate([kdt, bt], axis=0).astype(BF16)
        sc = _nt(lhs, rhs)
        m_k = jnp.where(strict, sc[:PAIR, :PAIR], 0.0)
        m_b = jnp.where(strict, sc[:PAIR, PAIR:], 0.0)
        a_k = jnp.where(incl, sc[PAIR:, :PAIR], 0.0)
        a_b = jnp.where(incl, sc[PAIR:, PAIR:], 0.0)

        t_inv = jnp.where(eye, 1.0, 0.0) - jnp.where(halves[0], m_b, 0.0)
        for lvl in range(1, len(halves)):
            t_inv = t_inv - _mm(_mm(t_inv, jnp.where(halves[lvl], m_b, 0.0)), t_inv)

        mkv = jnp.dot(m_k.astype(BF16), vs, preferred_element_type=F32)
        wu = _mm(t_inv, jnp.concatenate([kt, mkv], axis=1)).astype(BF16)
        wub = _tn(wu, bh)
        q_mat = _tn(vs, kh) - wub[PAIR:]
        abwu = jnp.dot(a_b.astype(BF16), wu, preferred_element_type=F32)
        r_eff = rt - abwu[:, :PAIR]
        y0 = jnp.dot(a_k.astype(BF16), vs, preferred_element_type=F32) - abwu[:, PAIR:]

        s0 = s_ref[pr]
        s0b = s0.astype(BF16)
        y = _nt(r_eff.astype(BF16), s0b) + y0
        s_ref[pr] = s0 * jnp.exp(tot_p) + (q_mat - jnp.dot(s0b, wub[:PAIR].astype(BF16), preferred_element_type=F32))
        y_ref[0, :, sl] = y[:c_len] + y[c_len:]


def _rwkv_scan(r, v, kk, kd, b, w, *, batch, seq, n_ctx):
    c_len = SCAN_CHUNK
    n_c = seq // c_len
    n_cc = n_ctx // c_len
    m = batch * seq

    def blk(bi, d, c):
        rev = jnp.where(c < n_cc, n_cc - 1 - c, n_c - 1 - (c - n_cc))
        return bi * n_c + jnp.where(d == 0, c, rev)

    tok = pl.BlockSpec((c_len, D_MODEL), lambda bi, d, c: (blk(bi, d, c), 0))
    dirs = pl.BlockSpec((1, c_len, D_MODEL), lambda bi, d, c: (d, blk(bi, d, c), 0))
    return pl.pallas_call(
        _scan_kernel,
        grid=(batch, 2, n_c),
        in_specs=[tok, tok, tok, dirs, dirs, dirs],
        out_specs=dirs,
        out_shape=jax.ShapeDtypeStruct((2, m, D_MODEL), F32),
        scratch_shapes=[pltpu.VMEM((D_MODEL // PAIR, PAIR, PAIR), F32)],
        compiler_params=_cparams(("parallel", "parallel", "arbitrary")),
        name="rwkv_scan",
    )(r, v, kk, kd, b, w)


def _mix_kernel(z_ref, mod_ref, att_ref, cv_ref, y0_ref, y1_ref, g_ref, bonus_ref, gates_ref,
                woa_ref, wpw_ref, wor_ref, wout_ref, lng_ref, lnb_ref, e_ref, et_ref, o_ref,
                *, tm, tiles, n_ctx, ctx_row):
    i = pl.program_id(0)
    y_a = jnp.dot(att_ref[...], woa_ref[...], preferred_element_type=F32)
    y_b = jnp.dot(cv_ref[...], wpw_ref[...], preferred_element_type=F32)

    y = y0_ref[0] + y1_ref[0]
    inv_n = 1.0 / RWKV_HEAD
    mu = _head_sum_bcast(y, e_ref, et_ref) * inv_n
    dlt = y - mu
    var = _head_sum_bcast(dlt * dlt, e_ref, et_ref) * inv_n
    yn = dlt * lax.rsqrt(var + GN_EPS) * lng_ref[...] + lnb_ref[...]
    rw = (yn + bonus_ref[...].astype(F32)) * g_ref[...].astype(F32)
    y_c = jnp.dot(rw.astype(BF16), wor_ref[...], preferred_element_type=F32)

    def gate(j):
        return _sigmoid(gates_ref[:, j * D_MODEL:(j + 1) * D_MODEL].astype(F32))

    mix = gate(0) * y_a + gate(1) * y_b + gate(2) * y_c
    out = jnp.dot(mix.astype(BF16), wout_ref[...], preferred_element_type=F32)
    g_msa = _mod_rows(mod_ref, 2, i // tiles, (i % tiles) * tm, tm, n_ctx, ctx_row)
    o_ref[...] = z_ref[...] + g_msa * out


def _mix_out(z, mod, att, cv, y_dir, g, bonus, p, woa, wpw, wor, wout, lng, lnb, e, et, *, seq, n_ctx, ctx_row):
    m, d = z.shape
    tm = _pick_tile(seq, (320, 256, 128))
    tiles = seq // tm
    row = pl.BlockSpec((tm, d), lambda i: (i, 0))
    full = lambda a: pl.BlockSpec(a.shape, lambda i: (0,) * a.ndim)
    kfn = functools.partial(_mix_kernel, tm=tm, tiles=tiles, n_ctx=n_ctx, ctx_row=ctx_row)
    return pl.pallas_call(
        kfn,
        grid=(m // tm,),
        in_specs=[
            row, full(mod), row, row,
            pl.BlockSpec((1, tm, d), lambda i: (0, i, 0)),
            pl.BlockSpec((1, tm, d), lambda i: (1, i, 0)),
            row, row,
            pl.BlockSpec((tm, 3 * d), lambda i: (i, COL_GATES // (3 * d))),
            full(woa), full(wpw), full(wor), full(wout), full(lng), full(lnb), full(e), full(et),
        ],
        out_specs=row,
        out_shape=jax.ShapeDtypeStruct((m, d), F32),
        compiler_params=_cparams(("parallel",)),
        name="mix_out",
    )(z, mod, att, cv, y_dir, y_dir, g, bonus, p, woa, wpw, wor, wout, lng, lnb, e, et)


def _mlp_kernel(z_ref, mod_ref, g_ref, wu_ref, wd_ref, fg_ref, o_ref, h_ref, acc_ref,
                *, tm, tiles, n_ctx, ctx_row, final):
    i = pl.program_id(0)
    f = pl.program_id(1)
    batch, row0 = i // tiles, (i % tiles) * tm

    @pl.when(f == 0)
    def _():
        h = _adaln(z_ref[...], g_ref[...], mod_ref, 3, 4, batch, row0, tm, n_ctx, ctx_row)
        h_ref[...] = h.astype(BF16)
        acc_ref[...] = jnp.zeros_like(acc_ref)

    a = jnp.dot(h_ref[...], wu_ref[...], preferred_element_type=F32)
    a = jnp.square(jnp.maximum(a, 0.0))
    acc_ref[...] += jnp.dot(a.astype(BF16), wd_ref[...], preferred_element_type=F32)

    @pl.when(f == pl.num_programs(1) - 1)
    def _():
        out = z_ref[...] + _mod_rows(mod_ref, 5, batch, row0, tm, n_ctx, ctx_row) * acc_ref[...]
        if final:
            out = _rms(out, fg_ref[...])
        o_ref[...] = out


def _mlp(z, mod, g, wu, wd, fg, *, seq, n_ctx, ctx_row, final):
    m, d = z.shape
    ff = wu.shape[1]
    tm = _pick_tile(seq, (1280, 640, 320, 256))
    tf = 512
    kfn = functools.partial(_mlp_kernel, tm=tm, tiles=seq // tm, n_ctx=n_ctx, ctx_row=ctx_row, final=final)
    return pl.pallas_call(
        kfn,
        grid=(m // tm, ff // tf),
        in_specs=[
            pl.BlockSpec((tm, d), lambda i, f: (i, 0)),
            pl.BlockSpec(mod.shape, lambda i, f: (0, 0, 0)),
            pl.BlockSpec((1, d), lambda i, f: (0, 0)),
            pl.BlockSpec((d, tf), lambda i, f: (0, f)),
            pl.BlockSpec((tf, d), lambda i, f: (f, 0)),
            pl.BlockSpec((1, d), lambda i, f: (0, 0)),
        ],
        out_specs=pl.BlockSpec((tm, d), lambda i, f: (i, 0)),
        out_shape=jax.ShapeDtypeStruct((m, d), F32),
        scratch_shapes=[pltpu.VMEM((tm, d), BF16), pltpu.VMEM((tm, d), F32)],
        compiler_params=_cparams(("parallel", "arbitrary")),
        name="mlp",
    )(z, mod, g, wu, wd, fg)


def _layout_w_in(w_in):
    depth = w_in.shape[0]
    off_q = Q_LORA
    off_kv = off_q + KV_LORA + MLA_ROPE
    off_glu = off_kv + 2 * D_MODEL
    rw = off_glu
    rw_in = 3 * D_MODEL + 2 * DECAY_LORA + 2 * ICL_LORA + GATE_LORA
    off_rwkv = off_glu + rw_in
    seg = lambda a, b: w_in[:, :, a:b]
    k_rope = seg(Q_LORA + KV_LORA, off_kv)
    swap = jnp.arange(MLA_ROPE) ^ ROPE_FREQS
    z64 = jnp.zeros(w_in.shape[:2] + (LANES - MLA_ROPE,), w_in.dtype)
    parts = [
        seg(off_rwkv, off_rwkv + 3 * D_MODEL),
        seg(off_kv, off_kv + 2 * D_MODEL),
        seg(rw, rw + 3 * D_MODEL),
        seg(0, Q_LORA), seg(Q_LORA, Q_LORA + KV_LORA),
        k_rope, z64, k_rope[:, :, swap], z64,
        seg(rw + 3 * D_MODEL, rw + rw_in),
    ]
    w = jnp.concatenate(parts, axis=-1)
    pad = D_IN_PAD - w.shape[-1]
    return jnp.pad(w, ((0, 0), (0, 0), (0, pad))).astype(BF16)


def _layout_w_q(w_q_b):
    depth = w_q_b.shape[0]
    w = w_q_b.reshape(depth, Q_LORA, MLA_HEADS, MLA_NOPE + MLA_ROPE)
    nope, rope = w[..., :MLA_NOPE], w[..., MLA_NOPE:]
    swap = jnp.arange(MLA_ROPE) ^ ROPE_FREQS
    z64 = jnp.zeros(rope.shape[:-1] + (LANES - MLA_ROPE,), w.dtype)
    w = jnp.concatenate([nope, rope, z64, rope[..., swap], z64], axis=-1)
    return w.reshape(depth, Q_LORA, MLA_HEADS * 3 * LANES).astype(BF16)


def _rope_tables(n_ctx, n_lat):
    rows = n_lat // GRID_W
    row = jnp.repeat(jnp.arange(rows), GRID_W).astype(F32)
    colp = jnp.tile(jnp.arange(GRID_W), rows).astype(F32)
    inv_freq = ROPE_THETA ** (-jnp.arange(ROPE_FREQS, dtype=F32) / ROPE_FREQS)
    ang_lat = jnp.stack([row[:, None] * inv_freq, colp[:, None] * inv_freq], axis=1)
    ang = jnp.concatenate([jnp.zeros((n_ctx, 2, ROPE_FREQS), F32), ang_lat], axis=0)
    cos, sin = jnp.cos(ang), jnp.sin(ang)
    cos_t = jnp.stack([cos, cos], axis=2).reshape(-1, MLA_ROPE)
    sin_t = jnp.stack([-sin, sin], axis=2).reshape(-1, MLA_ROPE)
    pad = ((0, 0), (0, LANES - MLA_ROPE))
    return jnp.pad(cos_t, pad), jnp.pad(sin_t, pad)


def _block_diag2(w):
    z = jnp.zeros_like(w[0])
    return jnp.concatenate([jnp.concatenate([w[0], z], axis=1), jnp.concatenate([z, w[1]], axis=1)], axis=0)


def kernel(x, c, ctx, c_ctx, w_mod, b_mod, norm1_g, norm2_g, w_in, q_a_norm_g, w_q_b, kv_a_norm_g, w_kv_b, w_o_mla, dw_kernel, dw_bias, conv_ln_g, conv_ln_b, w_pw2, rwkv_mu, rwkv_w0, rwkv_w2, rwkv_a0, rwkv_a2, rwkv_g2, rwkv_k_k, rwkv_k_a, rwkv_r_k, rwkv_ln_g, rwkv_ln_b, w_o_rwkv, w_out, w_up, w_down, final_norm_g):
    batch, n_lat, d = x.shape
    n_ctx = ctx.shape[1]
    seq = n_ctx + n_lat
    depth = w_mod.shape[0]
    assert d == D_MODEL and batch < 8 and n_ctx % SCAN_CHUNK == 0 and seq % SCAN_CHUNK == 0
    ctx_row = batch

    xc = jnp.zeros((8, d), F32).at[:batch].set(c).at[batch].set(c_ctx)
    mod_all = _modulation(xc, w_mod, b_mod).reshape(depth, 8, 6, d).transpose(0, 2, 1, 3)

    w_in_l = _layout_w_in(w_in)
    w_q_l = _layout_w_q(w_q_b)
    w_kv_l = w_kv_b.astype(BF16)
    cos_t, sin_t = _rope_tables(n_ctx, n_lat)
    head_of = jnp.arange(D_MODEL) // RWKV_HEAD
    e = (head_of[:, None] == jnp.arange(LANES)[None, :]).astype(F32)
    et = e.T
    rw_cols = 3 * D_MODEL
    row2 = lambda a: a.reshape(1, -1)
    kern_pad = jnp.pad(dw_kernel, ((0, 0), (0, 32 - CONV_W), (0, 0)))

    z = jnp.concatenate([ctx, x], axis=1).reshape(batch * seq, d)
    for l in range(depth):
        mod = mod_all[l]
        p = _in_proj(z, mod, row2(norm1_g[l]), w_in_l[l], seq=seq, n_ctx=n_ctx, ctx_row=ctx_row)

        q, k, v = _qkv(p, cos_t, sin_t, row2(q_a_norm_g[l]), row2(kv_a_norm_g[l]), w_q_l[l], w_kv_l[l],
                       batch=batch, seq=seq)
        att = _attention(q, k, v, n_ctx=n_ctx).reshape(batch * seq, MLA_HEADS * MLA_V)

        cv = _conv(p, kern_pad[l], row2(dw_bias[l]), row2(conv_ln_g[l]), row2(conv_ln_b[l]),
                   batch=batch, seq=seq, n_ctx=n_ctx)

        r_s, v_s, kk_s, kd_s, b_s, w_s, g_s, bonus = _rwkv_prep(
            p, rwkv_mu[l][:, :rw_cols], rwkv_mu[l][:, rw_cols:],
            rwkv_w0[l].reshape(1, 2 * d), rwkv_a0[l].reshape(1, 2 * d),
            _block_diag2(rwkv_w2[l]).astype(BF16), _block_diag2(rwkv_a2[l]).astype(BF16),
            rwkv_g2[l].astype(BF16), row2(rwkv_k_k[l]), row2(rwkv_k_a[l]), rwkv_r_k[l].reshape(1, d),
            e, et, batch=batch, seq=seq, n_ctx=n_ctx)
        y_dir = _rwkv_scan(r_s, v_s, kk_s, kd_s, b_s, w_s, batch=batch, seq=seq, n_ctx=n_ctx)

        z = _mix_out(z, mod, att, cv, y_dir, g_s, bonus, p,
                     w_o_mla[l].astype(BF16), w_pw2[l].astype(BF16), w_o_rwkv[l].astype(BF16),
                     w_out[l].astype(BF16), row2(rwkv_ln_g[l]), row2(rwkv_ln_b[l]), e, et,
                     seq=seq, n_ctx=n_ctx, ctx_row=ctx_row)

        z = _mlp(z, mod, row2(norm2_g[l]), w_up[l].astype(BF16), w_down[l].astype(BF16),
                 row2(final_norm_g), seq=seq, n_ctx=n_ctx, ctx_row=ctx_row, final=(l == depth - 1))

    return z.reshape(batch, seq, d)[:, n_ctx:]
```

```python
import functools
import math

import jax
import jax.numpy as jnp
from jax import lax
from jax.experimental import pallas as pl
from jax.experimental.pallas import tpu as pltpu

F32 = jnp.float32
BF16 = jnp.bfloat16
HIGHEST = lax.Precision.HIGHEST

D_MODEL = 1024
GRID_W = 64
NORM_EPS = 1e-6
LN_EPS = 1e-5
MLA_HEADS = 8
MLA_NOPE = 128
MLA_ROPE = 64
MLA_V = 128
Q_LORA = 512
KV_LORA = 256
ROPE_FREQS = MLA_ROPE // 4
ROPE_THETA = 10000.0
CONV_W = 31
RWKV_HEAD = 64
RWKV_HEADS = D_MODEL // RWKV_HEAD
DECAY_LORA = 64
ICL_LORA = 64
GATE_LORA = 128
GN_EPS = 64e-5
D_FF = 4 * D_MODEL

LANES = 128
BF16_SUBLANES = 16
VMEM_LIMIT = 56 * 1024 * 1024

COL_GATES = 0
COL_GLU_A = 3 * D_MODEL
COL_GLU_G = 4 * D_MODEL
COL_R = 5 * D_MODEL
COL_K = 6 * D_MODEL
COL_V = 7 * D_MODEL
COL_QA = 8 * D_MODEL
COL_KVL = COL_QA + Q_LORA
COL_KR = COL_KVL + KV_LORA
COL_LORA = COL_KR + 2 * LANES
D_IN_PAD = 19 * 512
HEAD_QK = 2 * LANES

SCAN_CHUNK = 64
PAIR = 2 * RWKV_HEAD


def _cparams(sem):
    return pltpu.CompilerParams(dimension_semantics=sem, vmem_limit_bytes=VMEM_LIMIT)


def _pick_tile(n, prefs):
    for t in prefs:
        if n % t == 0:
            return t
    raise ValueError(f"no tile in {prefs} divides {n}")


def _sigmoid(x):
    return 1.0 / (1.0 + jnp.exp(-x))


def _mod_kernel(x_ref, w_ref, b_ref, o_ref):
    x = x_ref[...]
    s = x * _sigmoid(x)
    o_ref[0] = jnp.dot(s, w_ref[0], precision=HIGHEST, preferred_element_type=F32) + b_ref[0]


def _modulation(xc, w_mod, b_mod):
    depth, d, n = w_mod.shape
    tn = _pick_tile(n, (1536, 1024, 512, 128))
    return pl.pallas_call(
        _mod_kernel,
        grid=(depth, n // tn),
        in_specs=[
            pl.BlockSpec((8, d), lambda l, j: (0, 0)),
            pl.BlockSpec((1, d, tn), lambda l, j: (l, 0, j)),
            pl.BlockSpec((1, 1, tn), lambda l, j: (l, 0, j)),
        ],
        out_specs=pl.BlockSpec((1, 8, tn), lambda l, j: (l, 0, j)),
        out_shape=jax.ShapeDtypeStruct((depth, 8, n), F32),
        compiler_params=_cparams(("parallel", "parallel")),
        name="modulation",
    )(xc, w_mod, b_mod.reshape(depth, 1, n))


def _mod_rows(mod_ref, slot, batch, row0, tm, n_ctx, ctx_row):
    rows = row0 + lax.broadcasted_iota(jnp.int32, (tm, 1), 0)
    lat = mod_ref[slot, pl.ds(batch, 1), :]
    ctx = mod_ref[slot, pl.ds(ctx_row, 1), :]
    return jnp.where(rows < n_ctx, ctx, lat)


def _rms(x, g):
    return x * lax.rsqrt(jnp.mean(x * x, axis=-1, keepdims=True) + NORM_EPS) * g


def _adaln(z, g, mod_ref, s_shift, s_scale, batch, row0, tm, n_ctx, ctx_row):
    shift = _mod_rows(mod_ref, s_shift, batch, row0, tm, n_ctx, ctx_row)
    scale = _mod_rows(mod_ref, s_scale, batch, row0, tm, n_ctx, ctx_row)
    return _rms(z, g) * (1.0 + scale) + shift


def _in_proj_kernel(z_ref, mod_ref, g_ref, w_ref, o_ref, h_ref, *, tm, tiles, n_ctx, ctx_row):
    i = pl.program_id(0)

    @pl.when(pl.program_id(1) == 0)
    def _():
        h = _adaln(z_ref[...], g_ref[...], mod_ref, 0, 1, i // tiles, (i % tiles) * tm, tm, n_ctx, ctx_row)
        h_ref[...] = h.astype(BF16)

    o_ref[...] = jnp.dot(h_ref[...], w_ref[...], preferred_element_type=F32).astype(o_ref.dtype)


def _in_proj(z, mod, g, w, *, seq, n_ctx, ctx_row):
    m, d = z.shape
    n = w.shape[1]
    tm = _pick_tile(seq, (1280, 640, 320, 256))
    tn = 512
    kern = functools.partial(_in_proj_kernel, tm=tm, tiles=seq // tm, n_ctx=n_ctx, ctx_row=ctx_row)
    return pl.pallas_call(
        kern,
        grid=(m // tm, n // tn),
        in_specs=[
            pl.BlockSpec((tm, d), lambda i, j: (i, 0)),
            pl.BlockSpec(mod.shape, lambda i, j: (0, 0, 0)),
            pl.BlockSpec((1, d), lambda i, j: (0, 0)),
            pl.BlockSpec((d, tn), lambda i, j: (0, j)),
        ],
        out_specs=pl.BlockSpec((tm, tn), lambda i, j: (i, j)),
        out_shape=jax.ShapeDtypeStruct((m, n), BF16),
        scratch_shapes=[pltpu.VMEM((tm, d), BF16)],
        compiler_params=_cparams(("parallel", "arbitrary")),
        name="in_proj",
    )(z, mod, g, w)


VT_ROWS = MLA_V + BF16_SUBLANES


def _qkv_kernel(qa_ref, kvl_ref, kr_ref, cos_ref, sin_ref, qg_ref, kvg_ref, wq_ref, wk_ref, wvt_ref,
                q_ref, k_ref, vt_ref, *, q_scale):
    cos = cos_ref[...]
    sin = sin_ref[...]
    qn = _rms(qa_ref[...].astype(F32), qg_ref[...]).astype(BF16)
    kvn = _rms(kvl_ref[...].astype(F32), kvg_ref[...]).astype(BF16)
    kr = kr_ref[...].astype(F32)
    k_rope = (kr[:, :LANES] * cos + kr[:, LANES:] * sin).astype(BF16)
    tm = qn.shape[0]
    ones_rows = jnp.where(lax.broadcasted_iota(jnp.int32, (BF16_SUBLANES, tm), 0) == 0, 1.0, 0.0).astype(BF16)
    for h in range(MLA_HEADS):
        qh = jnp.dot(qn, wq_ref[:, h * 3 * LANES:(h + 1) * 3 * LANES], preferred_element_type=F32)
        rope = qh[:, LANES:2 * LANES] * cos + qh[:, 2 * LANES:] * sin
        q_ref[0, h, :, :LANES] = (qh[:, :LANES] * q_scale).astype(BF16)
        q_ref[0, h, :, LANES:] = (rope * q_scale).astype(BF16)
        kh = jnp.dot(kvn, wk_ref[:, h * MLA_NOPE:(h + 1) * MLA_NOPE], preferred_element_type=F32)
        k_ref[0, h, :, :LANES] = kh.astype(BF16)
        k_ref[0, h, :, LANES:] = k_rope
        vt = _nt(wvt_ref[h * MLA_V:(h + 1) * MLA_V, :], kvn)
        vt_ref[0, h, :MLA_V, :] = vt.astype(BF16)
        vt_ref[0, h, MLA_V:, :] = ones_rows


def _qkv(p, cos_t, sin_t, qg, kvg, wq, wk, wvt, *, batch, seq):
    tm = _pick_tile(seq, (640, 256))
    tiles = seq // tm
    q_scale = (MLA_NOPE + MLA_ROPE) ** -0.5 * math.log2(math.e)
    kern = functools.partial(_qkv_kernel, q_scale=q_scale)
    row = lambda b, t: b * tiles + t
    return pl.pallas_call(
        kern,
        grid=(batch, tiles),
        in_specs=[
            pl.BlockSpec((tm, Q_LORA), lambda b, t: (row(b, t), COL_QA // Q_LORA)),
            pl.BlockSpec((tm, KV_LORA), lambda b, t: (row(b, t), COL_KVL // KV_LORA)),
            pl.BlockSpec((tm, 2 * LANES), lambda b, t: (row(b, t), COL_KR // (2 * LANES))),
            pl.BlockSpec((tm, LANES), lambda b, t: (t, 0)),
            pl.BlockSpec((tm, LANES), lambda b, t: (t, 0)),
            pl.BlockSpec((1, Q_LORA), lambda b, t: (0, 0)),
            pl.BlockSpec((1, KV_LORA), lambda b, t: (0, 0)),
            pl.BlockSpec(wq.shape, lambda b, t: (0, 0)),
            pl.BlockSpec(wk.shape, lambda b, t: (0, 0)),
            pl.BlockSpec(wvt.shape, lambda b, t: (0, 0)),
        ],
        out_specs=[
            pl.BlockSpec((1, MLA_HEADS, tm, HEAD_QK), lambda b, t: (b, 0, t, 0)),
            pl.BlockSpec((1, MLA_HEADS, tm, HEAD_QK), lambda b, t: (b, 0, t, 0)),
            pl.BlockSpec((1, MLA_HEADS, VT_ROWS, tm), lambda b, t: (b, 0, 0, t)),
        ],
        out_shape=[
            jax.ShapeDtypeStruct((batch, MLA_HEADS, seq, HEAD_QK), BF16),
            jax.ShapeDtypeStruct((batch, MLA_HEADS, seq, HEAD_QK), BF16),
            jax.ShapeDtypeStruct((batch, MLA_HEADS, VT_ROWS, seq), BF16),
        ],
        compiler_params=_cparams(("parallel", "parallel")),
        name="qkv",
    )(p, p, p, cos_t, sin_t, qg, kvg, wq, wk, wvt)


ATT_TQ = 256
NEG_BIG = -1e30


def _attn_kernel(q_ref, k_ref, vt_ref, o_ref, *, chains, tk, seq, n_ctx):
    tq = ATT_TQ
    key_limit = jnp.where(pl.program_id(2) == 0, n_ctx, seq)
    qs = [q_ref[0, 0, j * tq:(j + 1) * tq, :] for j in range(chains)]

    def body(c, carry):
        start = pl.multiple_of(c * tk, tk)
        kb = k_ref[0, 0, pl.ds(start, tk), :]
        vtb = vt_ref[0, 0, :, pl.ds(start, tk)]
        scores = [_nt(kb, qs[j]) for j in range(chains)]
        out = []
        for j in range(chains):
            m, acc = carry[j]
            s = scores[j]
            if j * tq < n_ctx:
                key = start + lax.broadcasted_iota(jnp.int32, (tk, 1), 0)
                s = jnp.where(key < key_limit, s, NEG_BIG)
            m_new = jnp.maximum(m, jnp.max(s, axis=0, keepdims=True))
            alpha = jnp.exp2(m - m_new)
            p = jnp.exp2(s - m_new).astype(BF16)
            acc = alpha * acc + jnp.dot(vtb, p, preferred_element_type=F32)
            out.append((m_new, acc))
        return tuple(out)

    init = tuple((jnp.full((1, tq), NEG_BIG, F32), jnp.zeros((VT_ROWS, tq), F32)) for _ in range(chains))
    final = lax.fori_loop(0, seq // tk, body, init)
    for j in range(chains):
        acc = final[j][1]
        o_t = acc[:MLA_V] / acc[MLA_V:MLA_V + 1]
        o_ref[0, j * tq:(j + 1) * tq, :] = o_t.T.astype(o_ref.dtype)


def _attention(q, k, vt, *, n_ctx):
    batch, heads, seq, _ = q.shape
    assert n_ctx % ATT_TQ == 0
    tile = _pick_tile(seq, (1280, 256))
    assert n_ctx <= tile
    tk = _pick_tile(seq, (1280, 256))
    kern = functools.partial(_attn_kernel, chains=tile // ATT_TQ, tk=tk, seq=seq, n_ctx=n_ctx)
    return pl.pallas_call(
        kern,
        grid=(batch, heads, seq // tile),
        in_specs=[
            pl.BlockSpec((1, 1, tile, HEAD_QK), lambda b, h, i: (b, h, i, 0)),
            pl.BlockSpec((1, 1, seq, HEAD_QK), lambda b, h, i: (b, h, 0, 0)),
            pl.BlockSpec((1, 1, VT_ROWS, seq), lambda b, h, i: (b, h, 0, 0)),
        ],
        out_specs=pl.BlockSpec((1, tile, MLA_V), lambda b, h, i: (b, i, h)),
        out_shape=jax.ShapeDtypeStruct((batch, seq, heads * MLA_V), BF16),
        compiler_params=_cparams(("parallel", "parallel", "arbitrary")),
        name="attention",
    )(q, k, vt)


HALO = BF16_SUBLANES


def _segment_edges(t, tm, tiles, n_ctx):
    start = t * tm
    has_prev = jnp.logical_and(start != 0, start != n_ctx)
    has_next = jnp.logical_and(start + tm != n_ctx, t != tiles - 1)
    return has_prev, has_next


def _conv_kernel(a_ref, g_ref, ap_ref, gp_ref, an_ref, gn_ref, kern_ref, bias_ref, lng_ref, lnb_ref,
                 o_ref, buf_ref, *, tm, tiles, n_ctx):
    t = pl.program_id(1)
    has_prev, has_next = _segment_edges(t, tm, tiles, n_ctx)

    def glu(a, g):
        return a[...].astype(F32) * _sigmoid(g[...].astype(F32))

    buf_ref[0:HALO, :] = jnp.where(has_prev, glu(ap_ref, gp_ref), 0.0)
    buf_ref[HALO:HALO + tm, :] = glu(a_ref, g_ref)
    buf_ref[HALO + tm:, :] = jnp.where(has_next, glu(an_ref, gn_ref), 0.0)

    half = CONV_W // 2
    rb = 32
    for r0 in range(0, tm, rb):
        acc = jnp.zeros((rb, D_MODEL), F32) + bias_ref[...]
        for j in range(CONV_W):
            acc = acc + buf_ref[r0 + HALO - half + j:r0 + HALO - half + j + rb, :] * kern_ref[j:j + 1, :]
        mu = jnp.mean(acc, axis=-1, keepdims=True)
        dlt = acc - mu
        var = jnp.mean(dlt * dlt, axis=-1, keepdims=True)
        y = dlt * lax.rsqrt(var + LN_EPS) * lng_ref[...] + lnb_ref[...]
        o_ref[r0:r0 + rb, :] = (y * _sigmoid(y)).astype(o_ref.dtype)


def _conv(p, kern, bias, lng, lnb, *, batch, seq, n_ctx):
    tm = _pick_tile(n_ctx, (256, 128))
    tiles = seq // tm
    per = tm // HALO
    last_halo = batch * seq // HALO - 1
    ca, cg = COL_GLU_A // D_MODEL, COL_GLU_G // D_MODEL
    cur = lambda c: (lambda b, t: (b * tiles + t, c))
    prev = lambda c: (lambda b, t: (jnp.maximum((b * tiles + t) * per - 1, 0), c))
    nxt = lambda c: (lambda b, t: (jnp.minimum((b * tiles + t + 1) * per, last_halo), c))
    full = lambda shape: pl.BlockSpec(shape, lambda b, t: (0, 0))
    kfn = functools.partial(_conv_kernel, tm=tm, tiles=tiles, n_ctx=n_ctx)
    return pl.pallas_call(
        kfn,
        grid=(batch, tiles),
        in_specs=[
            pl.BlockSpec((tm, D_MODEL), cur(ca)),
            pl.BlockSpec((tm, D_MODEL), cur(cg)),
            pl.BlockSpec((HALO, D_MODEL), prev(ca)),
            pl.BlockSpec((HALO, D_MODEL), prev(cg)),
            pl.BlockSpec((HALO, D_MODEL), nxt(ca)),
            pl.BlockSpec((HALO, D_MODEL), nxt(cg)),
            full(kern.shape), full(bias.shape), full(lng.shape), full(lnb.shape),
        ],
        out_specs=pl.BlockSpec((tm, D_MODEL), lambda b, t: (b * tiles + t, 0)),
        out_shape=jax.ShapeDtypeStruct((batch * seq, D_MODEL), BF16),
        scratch_shapes=[pltpu.VMEM((tm + 2 * HALO, D_MODEL), F32)],
        compiler_params=_cparams(("parallel", "parallel")),
        name="conv",
    )(p, p, p, p, p, p, kern, bias, lng, lnb)


def _head_sum_bcast(x, e_ref, et_ref):
    s = jnp.dot(x, e_ref[...], precision=HIGHEST, preferred_element_type=F32)
    return jnp.dot(s, et_ref[...], precision=HIGHEST, preferred_element_type=F32)


def _rwkv_prep_kernel(*refs, tm, tiles, n_ctx):
    (r_c, k_c, v_c, l_c, r_p, k_p, v_p, l_p, r_n, k_n, v_n, l_n,
     mu_rkv, mu_lora, w0_ref, a0_ref, w2_ref, a2_ref, g2_ref, kk_ref, ka_ref, rk_ref, e_ref, et_ref,
     r_o, v_o, kk_o, kd_o, b_o, w_o, g_o, bonus_o) = refs
    t = pl.program_id(1)
    has_prev, has_next = _segment_edges(t, tm, tiles, n_ctx)
    row = lax.broadcasted_iota(jnp.int32, (tm, 1), 0)

    def shifted(cur, prv, nxt, mu0, mu1):
        x = cur[...].astype(F32)
        before = jnp.where(has_prev, prv[HALO - 1:HALO, :].astype(F32), 0.0)
        after = jnp.where(has_next, nxt[0:1, :].astype(F32), 0.0)
        x_prev = jnp.where(row == 0, before, pltpu.roll(x, 1, 0))
        x_next = jnp.where(row == tm - 1, after, pltpu.roll(x, tm - 1, 0))
        return x + mu0 * (x_prev - x) + mu1 * (x_next - x)

    def col(ref, i):
        return ref[:, i * D_MODEL:(i + 1) * D_MODEL]

    r = shifted(r_c, r_p, r_n, mu_rkv[0:1, 0:D_MODEL], mu_rkv[1:2, 0:D_MODEL])
    k = shifted(k_c, k_p, k_n, mu_rkv[0:1, D_MODEL:2 * D_MODEL], mu_rkv[1:2, D_MODEL:2 * D_MODEL])
    v = shifted(v_c, v_p, v_n, mu_rkv[0:1, 2 * D_MODEL:], mu_rkv[1:2, 2 * D_MODEL:])
    lo = shifted(l_c, l_p, l_n, mu_lora[0:1, :], mu_lora[1:2, :])
    w1, a1, g1 = lo[:, :LANES], lo[:, LANES:2 * LANES], lo[:, 2 * LANES:]

    lw = jnp.dot(jnp.tanh(w1).astype(BF16), w2_ref[...], preferred_element_type=F32)
    la = jnp.dot(a1.astype(BF16), a2_ref[...], preferred_element_type=F32)
    g = jnp.dot(_sigmoid(g1).astype(BF16), g2_ref[...], preferred_element_type=F32)

    kk = k * kk_ref[...]
    kk = kk * lax.rsqrt(jnp.maximum(_head_sum_bcast(kk * kk, e_ref, et_ref), 1e-24))
    kd_sum = jnp.zeros_like(k)
    for d in range(2):
        x = -(col(w0_ref, d) + col(lw, d))
        softplus = jnp.maximum(x, 0.0) + jnp.log(1.0 + jnp.exp(-jnp.abs(x)))
        w_o[d] = -jnp.exp(-softplus - 0.5)
        a = _sigmoid(col(a0_ref, d) + col(la, d))
        kd = k * (1.0 + (a - 1.0) * ka_ref[...])
        kd_sum = kd_sum + kd
        kd_o[d] = kd.astype(BF16)
        b_o[d] = (kk * a).astype(BF16)
    bonus = _head_sum_bcast(r * kd_sum * rk_ref[...], e_ref, et_ref) * v
    r_o[...] = r.astype(BF16)
    v_o[...] = v.astype(BF16)
    kk_o[...] = kk.astype(BF16)
    g_o[...] = g.astype(BF16)
    bonus_o[...] = bonus.astype(BF16)


def _rwkv_prep(p, mu_rkv, mu_lora, w0, a0, w2, a2, g2, k_k, k_a, r_k, e, et, *, batch, seq, n_ctx):
    tm = _pick_tile(n_ctx, (256, 128))
    tiles = seq // tm
    per = tm // HALO
    m = batch * seq
    last_halo = m // HALO - 1
    cur = lambda c: (lambda b, t: (b * tiles + t, c))
    prev = lambda c: (lambda b, t: (jnp.maximum((b * tiles + t) * per - 1, 0), c))
    nxt = lambda c: (lambda b, t: (jnp.minimum((b * tiles + t + 1) * per, last_halo), c))
    lora_w = 3 * LANES
    cols = [(D_MODEL, COL_R // D_MODEL), (D_MODEL, COL_K // D_MODEL), (D_MODEL, COL_V // D_MODEL),
            (lora_w, COL_LORA // lora_w)]
    in_specs = [pl.BlockSpec((tm, w), cur(c)) for w, c in cols]
    in_specs += [pl.BlockSpec((HALO, w), prev(c)) for w, c in cols]
    in_specs += [pl.BlockSpec((HALO, w), nxt(c)) for w, c in cols]
    consts = [mu_rkv, mu_lora, w0, a0, w2, a2, g2, k_k, k_a, r_k, e, et]
    in_specs += [pl.BlockSpec(c.shape, lambda b, t: (0, 0)) for c in consts]
    row_spec = pl.BlockSpec((tm, D_MODEL), lambda b, t: (b * tiles + t, 0))
    dir_spec = pl.BlockSpec((2, tm, D_MODEL), lambda b, t: (0, b * tiles + t, 0))
    tok = lambda dt: jax.ShapeDtypeStruct((m, D_MODEL), dt)
    dirs = lambda dt: jax.ShapeDtypeStruct((2, m, D_MODEL), dt)
    kfn = functools.partial(_rwkv_prep_kernel, tm=tm, tiles=tiles, n_ctx=n_ctx)
    return pl.pallas_call(
        kfn,
        grid=(batch, tiles),
        in_specs=in_specs,
        out_specs=[row_spec, row_spec, row_spec, dir_spec, dir_spec, dir_spec, row_spec, row_spec],
        out_shape=[tok(BF16), tok(BF16), tok(BF16), dirs(BF16), dirs(BF16), dirs(F32), tok(BF16), tok(BF16)],
        compiler_params=_cparams(("parallel", "parallel")),
        name="rwkv_prep",
    )(*([p] * 12), *consts)


def _nt(a, b):
    return lax.dot_general(a, b, (((1,), (1,)), ((), ())), preferred_element_type=F32)


def _tn(a, b):
    return lax.dot_general(a, b, (((0,), (0,)), ((), ())), preferred_element_type=F32)


def _mm(a, b):
    return jnp.dot(a.astype(BF16), b.astype(BF16), preferred_element_type=F32)


def _scan_kernel(r_ref, v_ref, kk_ref, kd_ref, b_ref, w_ref, y_ref, s_ref):
    c_len = SCAN_CHUNK
    d = pl.program_id(1)

    @pl.when(pl.program_id(2) == 0)
    def _():
        s_ref[...] = jnp.zeros_like(s_ref)

    sign = 1 - 2 * d
    ri = lax.broadcasted_iota(jnp.int32, (c_len, c_len), 0)
    ci = lax.broadcasted_iota(jnp.int32, (c_len, c_len), 1)
    tri = jnp.where((ri - ci) * sign >= 0, 1.0, 0.0).astype(F32)

    w = w_ref[0]
    cum = jnp.dot(tri, w, precision=HIGHEST, preferred_element_type=F32)
    tot = jnp.sum(w, axis=0, keepdims=True)

    r2 = lax.broadcasted_iota(jnp.int32, (PAIR, PAIR), 0)
    c2 = lax.broadcasted_iota(jnp.int32, (PAIR, PAIR), 1)
    same = (r2 >> 6) == (c2 >> 6)
    dist = ((r2 & (c_len - 1)) - (c2 & (c_len - 1))) * sign
    strict = jnp.logical_and(same, dist > 0)
    incl = jnp.logical_and(same, dist >= 0)
    eye = jnp.where(r2 == c2, 1.0, 0.0)
    halves = [jnp.logical_and((r2 >> (l + 1)) == (c2 >> (l + 1)), ((r2 >> l) & 1) != ((c2 >> l) & 1))
              for l in range(6)]
    pairs = range(D_MODEL // PAIR)

    def stack(x):
        return jnp.where(same, jnp.concatenate([x, x], axis=0), 0.0)

    def dot(a, b):
        return jnp.dot(a, b, preferred_element_type=F32)

    rt, kt, vs, kh, bh, g_tot, sc = [], [], [], [], [], [], []
    for pr in pairs:
        sl = slice(pr * PAIR, (pr + 1) * PAIR)
        cum_p, w_p, tot_p = cum[:, sl], w[:, sl], tot[:, sl]
        g_inv = jnp.exp(-cum_p)
        g_tail = jnp.exp(tot_p - cum_p)
        kd = kd_ref[0, :, sl].astype(F32)
        b = b_ref[0, :, sl].astype(F32)
        rt.append(stack(r_ref[:, sl].astype(F32) * jnp.exp(cum_p)))
        kt.append(stack(kk_ref[:, sl].astype(F32) * jnp.exp(cum_p - w_p)).astype(BF16))
        vs.append(stack(v_ref[:, sl].astype(F32)).astype(BF16))
        kh.append(stack(kd * g_tail).astype(BF16))
        bh.append(stack(b * g_tail).astype(BF16))
        g_tot.append(jnp.exp(tot_p))
        lhs = jnp.concatenate([kt[pr], rt[pr].astype(BF16)], axis=0)
        rhs = jnp.concatenate([stack(kd * g_inv), stack(b * g_inv)], axis=0).astype(BF16)
        sc.append(_nt(lhs, rhs))

    m_b = [jnp.where(strict, sc[pr][:PAIR, PAIR:], 0.0) for pr in pairs]
    a_b = [jnp.where(incl, sc[pr][PAIR:, PAIR:], 0.0).astype(BF16) for pr in pairs]
    mkv = [dot(jnp.where(strict, sc[pr][:PAIR, :PAIR], 0.0).astype(BF16), vs[pr]) for pr in pairs]
    akv = [dot(jnp.where(incl, sc[pr][PAIR:, :PAIR], 0.0).astype(BF16), vs[pr]) for pr in pairs]
    vtk = [_tn(vs[pr], kh[pr]) for pr in pairs]

    t_inv = [eye - jnp.where(halves[0], m_b[pr], 0.0) for pr in pairs]
    for lvl in range(1, len(halves)):
        t_bf = [t_inv[pr].astype(BF16) for pr in pairs]
        dx = [dot(t_bf[pr], jnp.where(halves[lvl], m_b[pr], 0.0).astype(BF16)).astype(BF16) for pr in pairs]
        t_inv = [t_inv[pr] - dot(dx[pr], t_bf[pr]) for pr in pairs]

    wu = [dot(t_inv[pr].astype(BF16), jnp.concatenate([kt[pr], mkv[pr].astype(BF16)], axis=1)).astype(BF16)
          for pr in pairs]
    wub = [_tn(wu[pr], bh[pr]) for pr in pairs]
    abwu = [dot(a_b[pr], wu[pr]) for pr in pairs]
    s0 = [s_ref[pr] for pr in pairs]
    s0b = [s0[pr].astype(BF16) for pr in pairs]
    y = [_nt((rt[pr] - abwu[pr][:, :PAIR]).astype(BF16), s0b[pr]) for pr in pairs]
    ds = [dot(s0b[pr], wub[pr][:PAIR].astype(BF16)) for pr in pairs]
    for pr in pairs:
        sl = slice(pr * PAIR, (pr + 1) * PAIR)
        s_ref[pr] = s0[pr] * g_tot[pr] + (vtk[pr] - wub[pr][PAIR:] - ds[pr])
        y_st = y[pr] + akv[pr] - abwu[pr][:, PAIR:]
        y_ref[0, :, sl] = y_st[:c_len] + y_st[c_len:]


def _rwkv_scan(r, v, kk, kd, b, w, *, batch, seq, n_ctx):
    c_len = SCAN_CHUNK
    n_c = seq // c_len
    n_cc = n_ctx // c_len
    m = batch * seq

    def blk(bi, d, c):
        rev = jnp.where(c < n_cc, n_cc - 1 - c, n_c - 1 - (c - n_cc))
        return bi * n_c + jnp.where(d == 0, c, rev)

    tok = pl.BlockSpec((c_len, D_MODEL), lambda bi, d, c: (blk(bi, d, c), 0))
    dirs = pl.BlockSpec((1, c_len, D_MODEL), lambda bi, d, c: (d, blk(bi, d, c), 0))
    return pl.pallas_call(
        _scan_kernel,
        grid=(batch, 2, n_c),
        in_specs=[tok, tok, tok, dirs, dirs, dirs],
        out_specs=dirs,
        out_shape=jax.ShapeDtypeStruct((2, m, D_MODEL), F32),
        scratch_shapes=[pltpu.VMEM((D_MODEL // PAIR, PAIR, PAIR), F32)],
        compiler_params=_cparams(("parallel", "parallel", "arbitrary")),
        name="rwkv_scan",
    )(r, v, kk, kd, b, w)


def _mix_kernel(z_ref, mod_ref, att_ref, cv_ref, y0_ref, y1_ref, g_ref, bonus_ref, gates_ref,
                woa_ref, wpw_ref, wor_ref, wout_ref, lng_ref, lnb_ref, e_ref, et_ref, o_ref,
                *, tm, tiles, n_ctx, ctx_row):
    i = pl.program_id(0)
    y_a = jnp.dot(att_ref[...], woa_ref[...], preferred_element_type=F32)
    y_b = jnp.dot(cv_ref[...], wpw_ref[...], preferred_element_type=F32)

    y = y0_ref[0] + y1_ref[0]
    inv_n = 1.0 / RWKV_HEAD
    mu = _head_sum_bcast(y, e_ref, et_ref) * inv_n
    dlt = y - mu
    var = _head_sum_bcast(dlt * dlt, e_ref, et_ref) * inv_n
    yn = dlt * lax.rsqrt(var + GN_EPS) * lng_ref[...] + lnb_ref[...]
    rw = (yn + bonus_ref[...].astype(F32)) * g_ref[...].astype(F32)
    y_c = jnp.dot(rw.astype(BF16), wor_ref[...], preferred_element_type=F32)

    def gate(j):
        return _sigmoid(gates_ref[:, j * D_MODEL:(j + 1) * D_MODEL].astype(F32))

    mix = gate(0) * y_a + gate(1) * y_b + gate(2) * y_c
    out = jnp.dot(mix.astype(BF16), wout_ref[...], preferred_element_type=F32)
    g_msa = _mod_rows(mod_ref, 2, i // tiles, (i % tiles) * tm, tm, n_ctx, ctx_row)
    o_ref[...] = z_ref[...] + g_msa * out


def _mix_out(z, mod, att, cv, y_dir, g, bonus, p, woa, wpw, wor, wout, lng, lnb, e, et, *, seq, n_ctx, ctx_row):
    m, d = z.shape
    tm = _pick_tile(seq, (320, 256, 128))
    tiles = seq // tm
    row = pl.BlockSpec((tm, d), lambda i: (i, 0))
    full = lambda a: pl.BlockSpec(a.shape, lambda i: (0,) * a.ndim)
    kfn = functools.partial(_mix_kernel, tm=tm, tiles=tiles, n_ctx=n_ctx, ctx_row=ctx_row)
    return pl.pallas_call(
        kfn,
        grid=(m // tm,),
        in_specs=[
            row, full(mod), row, row,
            pl.BlockSpec((1, tm, d), lambda i: (0, i, 0)),
            pl.BlockSpec((1, tm, d), lambda i: (1, i, 0)),
            row, row,
            pl.BlockSpec((tm, 3 * d), lambda i: (i, COL_GATES // (3 * d))),
            full(woa), full(wpw), full(wor), full(wout), full(lng), full(lnb), full(e), full(et),
        ],
        out_specs=row,
        out_shape=jax.ShapeDtypeStruct((m, d), F32),
        compiler_params=_cparams(("parallel",)),
        name="mix_out",
    )(z, mod, att, cv, y_dir, y_dir, g, bonus, p, woa, wpw, wor, wout, lng, lnb, e, et)


def _mlp_kernel(z_ref, mod_ref, g_ref, wu_ref, wd_ref, fg_ref, o_ref, h_ref, acc_ref,
                *, tm, tiles, n_ctx, ctx_row, final):
    i = pl.program_id(0)
    f = pl.program_id(1)
    batch, row0 = i // tiles, (i % tiles) * tm

    @pl.when(f == 0)
    def _():
        h = _adaln(z_ref[...], g_ref[...], mod_ref, 3, 4, batch, row0, tm, n_ctx, ctx_row)
        h_ref[...] = h.astype(BF16)
        acc_ref[...] = jnp.zeros_like(acc_ref)

    a = jnp.dot(h_ref[...], wu_ref[...], preferred_element_type=F32)
    a = jnp.square(jnp.maximum(a, 0.0))
    acc_ref[...] += jnp.dot(a.astype(BF16), wd_ref[...], preferred_element_type=F32)

    @pl.when(f == pl.num_programs(1) - 1)
    def _():
        out = z_ref[...] + _mod_rows(mod_ref, 5, batch, row0, tm, n_ctx, ctx_row) * acc_ref[...]
        if final:
            out = _rms(out, fg_ref[...])
        o_ref[...] = out


def _mlp(z, mod, g, wu, wd, fg, *, seq, n_ctx, ctx_row, final):
    m, d = z.shape
    ff = wu.shape[1]
    tm = _pick_tile(seq, (1280, 640, 320, 256))
    tf = 512
    kfn = functools.partial(_mlp_kernel, tm=tm, tiles=seq // tm, n_ctx=n_ctx, ctx_row=ctx_row, final=final)
    return pl.pallas_call(
        kfn,
        grid=(m // tm, ff // tf),
        in_specs=[
            pl.BlockSpec((tm, d), lambda i, f: (i, 0)),
            pl.BlockSpec(mod.shape, lambda i, f: (0, 0, 0)),
            pl.BlockSpec((1, d), lambda i, f: (0, 0)),
            pl.BlockSpec((d, tf), lambda i, f: (0, f)),
            pl.BlockSpec((tf, d), lambda i, f: (f, 0)),
            pl.BlockSpec((1, d), lambda i, f: (0, 0)),
        ],
        out_specs=pl.BlockSpec((tm, d), lambda i, f: (i, 0)),
        out_shape=jax.ShapeDtypeStruct((m, d), F32),
        scratch_shapes=[pltpu.VMEM((tm, d), BF16), pltpu.VMEM((tm, d), F32)],
        compiler_params=_cparams(("parallel", "arbitrary")),
        name="mlp",
    )(z, mod, g, wu, wd, fg)


def _layout_w_in(w_in):
    depth = w_in.shape[0]
    off_q = Q_LORA
    off_kv = off_q + KV_LORA + MLA_ROPE
    off_glu = off_kv + 2 * D_MODEL
    rw = off_glu
    rw_in = 3 * D_MODEL + 2 * DECAY_LORA + 2 * ICL_LORA + GATE_LORA
    off_rwkv = off_glu + rw_in
    seg = lambda a, b: w_in[:, :, a:b]
    k_rope = seg(Q_LORA + KV_LORA, off_kv)
    swap = jnp.arange(MLA_ROPE) ^ ROPE_FREQS
    z64 = jnp.zeros(w_in.shape[:2] + (LANES - MLA_ROPE,), w_in.dtype)
    parts = [
        seg(off_rwkv, off_rwkv + 3 * D_MODEL),
        seg(off_kv, off_kv + 2 * D_MODEL),
        seg(rw, rw + 3 * D_MODEL),
        seg(0, Q_LORA), seg(Q_LORA, Q_LORA + KV_LORA),
        k_rope, z64, k_rope[:, :, swap], z64,
        seg(rw + 3 * D_MODEL, rw + rw_in),
    ]
    w = jnp.concatenate(parts, axis=-1)
    pad = D_IN_PAD - w.shape[-1]
    return jnp.pad(w, ((0, 0), (0, 0), (0, pad))).astype(BF16)


def _layout_w_q(w_q_b):
    depth = w_q_b.shape[0]
    w = w_q_b.reshape(depth, Q_LORA, MLA_HEADS, MLA_NOPE + MLA_ROPE)
    nope, rope = w[..., :MLA_NOPE], w[..., MLA_NOPE:]
    swap = jnp.arange(MLA_ROPE) ^ ROPE_FREQS
    z64 = jnp.zeros(rope.shape[:-1] + (LANES - MLA_ROPE,), w.dtype)
    w = jnp.concatenate([nope, rope, z64, rope[..., swap], z64], axis=-1)
    return w.reshape(depth, Q_LORA, MLA_HEADS * 3 * LANES).astype(BF16)


def _rope_tables(n_ctx, n_lat):
    rows = n_lat // GRID_W
    row = jnp.repeat(jnp.arange(rows), GRID_W).astype(F32)
    colp = jnp.tile(jnp.arange(GRID_W), rows).astype(F32)
    inv_freq = ROPE_THETA ** (-jnp.arange(ROPE_FREQS, dtype=F32) / ROPE_FREQS)
    ang_lat = jnp.stack([row[:, None] * inv_freq, colp[:, None] * inv_freq], axis=1)
    ang = jnp.concatenate([jnp.zeros((n_ctx, 2, ROPE_FREQS), F32), ang_lat], axis=0)
    cos, sin = jnp.cos(ang), jnp.sin(ang)
    cos_t = jnp.stack([cos, cos], axis=2).reshape(-1, MLA_ROPE)
    sin_t = jnp.stack([-sin, sin], axis=2).reshape(-1, MLA_ROPE)
    pad = ((0, 0), (0, LANES - MLA_ROPE))
    return jnp.pad(cos_t, pad), jnp.pad(sin_t, pad)


def _block_diag2(w):
    z = jnp.zeros_like(w[0])
    return jnp.concatenate([jnp.concatenate([w[0], z], axis=1), jnp.concatenate([z, w[1]], axis=1)], axis=0)


def kernel(x, c, ctx, c_ctx, w_mod, b_mod, norm1_g, norm2_g, w_in, q_a_norm_g, w_q_b, kv_a_norm_g, w_kv_b, w_o_mla, dw_kernel, dw_bias, conv_ln_g, conv_ln_b, w_pw2, rwkv_mu, rwkv_w0, rwkv_w2, rwkv_a0, rwkv_a2, rwkv_g2, rwkv_k_k, rwkv_k_a, rwkv_r_k, rwkv_ln_g, rwkv_ln_b, w_o_rwkv, w_out, w_up, w_down, final_norm_g):
    batch, n_lat, d = x.shape
    n_ctx = ctx.shape[1]
    seq = n_ctx + n_lat
    depth = w_mod.shape[0]
    assert d == D_MODEL and batch < 8 and n_ctx % SCAN_CHUNK == 0 and seq % SCAN_CHUNK == 0
    ctx_row = batch

    xc = jnp.zeros((8, d), F32).at[:batch].set(c).at[batch].set(c_ctx)
    mod_all = _modulation(xc, w_mod, b_mod).reshape(depth, 8, 6, d).transpose(0, 2, 1, 3)

    w_in_l = _layout_w_in(w_in)
    w_q_l = _layout_w_q(w_q_b)
    w_kv_h = w_kv_b.reshape(depth, KV_LORA, MLA_HEADS, MLA_NOPE + MLA_V)
    w_k_l = w_kv_h[..., :MLA_NOPE].reshape(depth, KV_LORA, MLA_HEADS * MLA_NOPE).astype(BF16)
    w_vt_l = w_kv_h[..., MLA_NOPE:].transpose(0, 2, 3, 1).reshape(depth, MLA_HEADS * MLA_V, KV_LORA).astype(BF16)
    cos_t, sin_t = _rope_tables(n_ctx, n_lat)
    head_of = jnp.arange(D_MODEL) // RWKV_HEAD
    e = (head_of[:, None] == jnp.arange(LANES)[None, :]).astype(F32)
    et = e.T
    rw_cols = 3 * D_MODEL
    row2 = lambda a: a.reshape(1, -1)
    kern_pad = jnp.pad(dw_kernel, ((0, 0), (0, 32 - CONV_W), (0, 0)))

    z = jnp.concatenate([ctx, x], axis=1).reshape(batch * seq, d)
    for l in range(depth):
        mod = mod_all[l]
        p = _in_proj(z, mod, row2(norm1_g[l]), w_in_l[l], seq=seq, n_ctx=n_ctx, ctx_row=ctx_row)

        q, k, vt = _qkv(p, cos_t, sin_t, row2(q_a_norm_g[l]), row2(kv_a_norm_g[l]), w_q_l[l], w_k_l[l], w_vt_l[l],
                        batch=batch, seq=seq)
        att = _attention(q, k, vt, n_ctx=n_ctx).reshape(batch * seq, MLA_HEADS * MLA_V)

        cv = _conv(p, kern_pad[l], row2(dw_bias[l]), row2(conv_ln_g[l]), row2(conv_ln_b[l]),
                   batch=batch, seq=seq, n_ctx=n_ctx)

        r_s, v_s, kk_s, kd_s, b_s, w_s, g_s, bonus = _rwkv_prep(
            p, rwkv_mu[l][:, :rw_cols], rwkv_mu[l][:, rw_cols:],
            rwkv_w0[l].reshape(1, 2 * d), rwkv_a0[l].reshape(1, 2 * d),
            _block_diag2(rwkv_w2[l]).astype(BF16), _block_diag2(rwkv_a2[l]).astype(BF16),
            rwkv_g2[l].astype(BF16), row2(rwkv_k_k[l]), row2(rwkv_k_a[l]), rwkv_r_k[l].reshape(1, d),
            e, et, batch=batch, seq=seq, n_ctx=n_ctx)
        y_dir = _rwkv_scan(r_s, v_s, kk_s, kd_s, b_s, w_s, batch=batch, seq=seq, n_ctx=n_ctx)

        z = _mix_out(z, mod, att, cv, y_dir, g_s, bonus, p,
                     w_o_mla[l].astype(BF16), w_pw2[l].astype(BF16), w_o_rwkv[l].astype(BF16),
                     w_out[l].astype(BF16), row2(rwkv_ln_g[l]), row2(rwkv_ln_b[l]), e, et,
                     seq=seq, n_ctx=n_ctx, ctx_row=ctx_row)

        z = _mlp(z, mod, row2(norm2_g[l]), w_up[l].astype(BF16), w_down[l].astype(BF16),
                 row2(final_norm_g), seq=seq, n_ctx=n_ctx, ctx_row=ctx_row, final=(l == depth - 1))

    return z.reshape(batch, seq, d)[:, n_ctx:]
```

```python
import functools
import math

import jax
import jax.numpy as jnp
from jax import lax
from jax.experimental import pallas as pl
from jax.experimental.pallas import tpu as pltpu

F32 = jnp.float32
BF16 = jnp.bfloat16
HIGHEST = lax.Precision.HIGHEST

D_MODEL = 1024
GRID_W = 64
NORM_EPS = 1e-6
LN_EPS = 1e-5
MLA_HEADS = 8
MLA_NOPE = 128
MLA_ROPE = 64
MLA_V = 128
Q_LORA = 512
KV_LORA = 256
ROPE_FREQS = MLA_ROPE // 4
ROPE_THETA = 10000.0
CONV_W = 31
RWKV_HEAD = 64
RWKV_HEADS = D_MODEL // RWKV_HEAD
DECAY_LORA = 64
ICL_LORA = 64
GATE_LORA = 128
GN_EPS = 64e-5
D_FF = 4 * D_MODEL

LANES = 128
BF16_SUBLANES = 16
VMEM_LIMIT = 56 * 1024 * 1024

COL_GATES = 0
COL_GLU_A = 3 * D_MODEL
COL_GLU_G = 4 * D_MODEL
COL_R = 5 * D_MODEL
COL_K = 6 * D_MODEL
COL_V = 7 * D_MODEL
COL_QA = 8 * D_MODEL
COL_KVL = COL_QA + Q_LORA
COL_KR = COL_KVL + KV_LORA
COL_LORA = COL_KR + 2 * LANES
D_IN_PAD = 19 * 512
HEAD_QK = 2 * LANES

SCAN_CHUNK = 64
PAIR = 2 * RWKV_HEAD


def _cparams(sem):
    return pltpu.CompilerParams(dimension_semantics=sem, vmem_limit_bytes=VMEM_LIMIT)


def _pick_tile(n, prefs):
    for t in prefs:
        if n % t == 0:
            return t
    raise ValueError(f"no tile in {prefs} divides {n}")


def _sigmoid(x):
    return 1.0 / (1.0 + jnp.exp(-x))


def _mod_kernel(x_ref, w_ref, b_ref, o_ref):
    x = x_ref[...]
    s = x * _sigmoid(x)
    o_ref[0] = jnp.dot(s, w_ref[0], precision=HIGHEST, preferred_element_type=F32) + b_ref[0]


def _modulation(xc, w_mod, b_mod):
    depth, d, n = w_mod.shape
    tn = _pick_tile(n, (1536, 1024, 512, 128))
    return pl.pallas_call(
        _mod_kernel,
        grid=(depth, n // tn),
        in_specs=[
            pl.BlockSpec((8, d), lambda l, j: (0, 0)),
            pl.BlockSpec((1, d, tn), lambda l, j: (l, 0, j)),
            pl.BlockSpec((1, 1, tn), lambda l, j: (l, 0, j)),
        ],
        out_specs=pl.BlockSpec((1, 8, tn), lambda l, j: (l, 0, j)),
        out_shape=jax.ShapeDtypeStruct((depth, 8, n), F32),
        compiler_params=_cparams(("parallel", "parallel")),
        name="modulation",
    )(xc, w_mod, b_mod.reshape(depth, 1, n))


def _mod_rows(mod_ref, slot, batch, row0, tm, n_ctx, ctx_row):
    rows = row0 + lax.broadcasted_iota(jnp.int32, (tm, 1), 0)
    lat = mod_ref[slot, pl.ds(batch, 1), :]
    ctx = mod_ref[slot, pl.ds(ctx_row, 1), :]
    return jnp.where(rows < n_ctx, ctx, lat)


def _rms(x, g):
    return x * lax.rsqrt(jnp.mean(x * x, axis=-1, keepdims=True) + NORM_EPS) * g


def _adaln(z, g, mod_ref, s_shift, s_scale, batch, row0, tm, n_ctx, ctx_row):
    shift = _mod_rows(mod_ref, s_shift, batch, row0, tm, n_ctx, ctx_row)
    scale = _mod_rows(mod_ref, s_scale, batch, row0, tm, n_ctx, ctx_row)
    return _rms(z, g) * (1.0 + scale) + shift


def _in_proj_kernel(z_ref, mod_ref, g_ref, w_ref, o_ref, h_ref, *, tm, tiles, n_ctx, ctx_row):
    i = pl.program_id(0)

    @pl.when(pl.program_id(1) == 0)
    def _():
        h = _adaln(z_ref[...], g_ref[...], mod_ref, 0, 1, i // tiles, (i % tiles) * tm, tm, n_ctx, ctx_row)
        h_ref[...] = h.astype(BF16)

    o_ref[...] = jnp.dot(h_ref[...], w_ref[...], preferred_element_type=F32).astype(o_ref.dtype)


def _in_proj(z, mod, g, w, *, seq, n_ctx, ctx_row):
    m, d = z.shape
    n = w.shape[1]
    tm = _pick_tile(seq, (1280, 640, 320, 256))
    tn = 512
    kern = functools.partial(_in_proj_kernel, tm=tm, tiles=seq // tm, n_ctx=n_ctx, ctx_row=ctx_row)
    return pl.pallas_call(
        kern,
        grid=(m // tm, n // tn),
        in_specs=[
            pl.BlockSpec((tm, d), lambda i, j: (i, 0)),
            pl.BlockSpec(mod.shape, lambda i, j: (0, 0, 0)),
            pl.BlockSpec((1, d), lambda i, j: (0, 0)),
            pl.BlockSpec((d, tn), lambda i, j: (0, j)),
        ],
        out_specs=pl.BlockSpec((tm, tn), lambda i, j: (i, j)),
        out_shape=jax.ShapeDtypeStruct((m, n), BF16),
        scratch_shapes=[pltpu.VMEM((tm, d), BF16)],
        compiler_params=_cparams(("parallel", "arbitrary")),
        name="in_proj",
    )(z, mod, g, w)


VT_ROWS = MLA_V + BF16_SUBLANES


def _qkv_kernel(qa_ref, kvl_ref, kr_ref, cos_ref, sin_ref, qg_ref, kvg_ref, wq_ref, wk_ref, wvt_ref,
                q_ref, k_ref, vt_ref, *, q_scale):
    cos = cos_ref[...]
    sin = sin_ref[...]
    qn = _rms(qa_ref[...].astype(F32), qg_ref[...]).astype(BF16)
    kvn = _rms(kvl_ref[...].astype(F32), kvg_ref[...]).astype(BF16)
    kr = kr_ref[...].astype(F32)
    k_rope = (kr[:, :LANES] * cos + kr[:, LANES:] * sin).astype(BF16)
    tm = qn.shape[0]
    ones_rows = jnp.where(lax.broadcasted_iota(jnp.int32, (BF16_SUBLANES, tm), 0) == 0, 1.0, 0.0).astype(BF16)
    for h in range(MLA_HEADS):
        qh = jnp.dot(qn, wq_ref[:, h * 3 * LANES:(h + 1) * 3 * LANES], preferred_element_type=F32)
        rope = qh[:, LANES:2 * LANES] * cos + qh[:, 2 * LANES:] * sin
        q_ref[0, h, :, :LANES] = (qh[:, :LANES] * q_scale).astype(BF16)
        q_ref[0, h, :, LANES:] = (rope * q_scale).astype(BF16)
        kh = jnp.dot(kvn, wk_ref[:, h * MLA_NOPE:(h + 1) * MLA_NOPE], preferred_element_type=F32)
        k_ref[0, h, :, :LANES] = kh.astype(BF16)
        k_ref[0, h, :, LANES:] = k_rope
        vt = _nt(wvt_ref[h * MLA_V:(h + 1) * MLA_V, :], kvn)
        vt_ref[0, h, :MLA_V, :] = vt.astype(BF16)
        vt_ref[0, h, MLA_V:, :] = ones_rows


def _qkv(p, cos_t, sin_t, qg, kvg, wq, wk, wvt, *, batch, seq):
    tm = _pick_tile(seq, (640, 256))
    tiles = seq // tm
    q_scale = (MLA_NOPE + MLA_ROPE) ** -0.5 * math.log2(math.e)
    kern = functools.partial(_qkv_kernel, q_scale=q_scale)
    row = lambda b, t: b * tiles + t
    return pl.pallas_call(
        kern,
        grid=(batch, tiles),
        in_specs=[
            pl.BlockSpec((tm, Q_LORA), lambda b, t: (row(b, t), COL_QA // Q_LORA)),
            pl.BlockSpec((tm, KV_LORA), lambda b, t: (row(b, t), COL_KVL // KV_LORA)),
            pl.BlockSpec((tm, 2 * LANES), lambda b, t: (row(b, t), COL_KR // (2 * LANES))),
            pl.BlockSpec((tm, LANES), lambda b, t: (t, 0)),
            pl.BlockSpec((tm, LANES), lambda b, t: (t, 0)),
            pl.BlockSpec((1, Q_LORA), lambda b, t: (0, 0)),
            pl.BlockSpec((1, KV_LORA), lambda b, t: (0, 0)),
            pl.BlockSpec(wq.shape, lambda b, t: (0, 0)),
            pl.BlockSpec(wk.shape, lambda b, t: (0, 0)),
            pl.BlockSpec(wvt.shape, lambda b, t: (0, 0)),
        ],
        out_specs=[
            pl.BlockSpec((1, MLA_HEADS, tm, HEAD_QK), lambda b, t: (b, 0, t, 0)),
            pl.BlockSpec((1, MLA_HEADS, tm, HEAD_QK), lambda b, t: (b, 0, t, 0)),
            pl.BlockSpec((1, MLA_HEADS, VT_ROWS, tm), lambda b, t: (b, 0, 0, t)),
        ],
        out_shape=[
            jax.ShapeDtypeStruct((batch, MLA_HEADS, seq, HEAD_QK), BF16),
            jax.ShapeDtypeStruct((batch, MLA_HEADS, seq, HEAD_QK), BF16),
            jax.ShapeDtypeStruct((batch, MLA_HEADS, VT_ROWS, seq), BF16),
        ],
        compiler_params=_cparams(("parallel", "parallel")),
        name="qkv",
    )(p, p, p, cos_t, sin_t, qg, kvg, wq, wk, wvt)


ATT_TQ = 256
NEG_BIG = -1e30


def _softmax_block(s, m, acc, vtb):
    m_new = jnp.maximum(m, jnp.max(s, axis=0, keepdims=True))
    alpha = jnp.exp2(m - m_new)
    p = jnp.exp2(s - m_new).astype(BF16)
    return m_new, alpha * acc + jnp.dot(vtb, p, preferred_element_type=F32)


def _attn_out(acc):
    return (acc[:MLA_V] / acc[MLA_V:MLA_V + 1]).T


def _attn_kernel(q_ref, k_ref, vt_ref, o_ref, s_ref, m_ref, acc_ref, *, chains, tk, seq):
    tq = ATT_TQ
    nblk = seq // tk

    def q_chain(j):
        return q_ref[0, 0, j * tq:(j + 1) * tq, :]

    def k_block(c):
        return k_ref[0, 0, pl.ds(pl.multiple_of(c * tk, tk), tk), :]

    def step(c, cur, prefetch):
        vtb = vt_ref[0, 0, :, pl.ds(pl.multiple_of(c * tk, tk), tk)]
        kb_next = k_block(c + 1) if prefetch else None
        for j in range(chains):
            s = s_ref[cur, j]
            m = m_ref[j]
            m_new = jnp.maximum(m, jnp.max(s, axis=0, keepdims=True))
            p = jnp.exp2(s - m_new).astype(BF16)
            if prefetch:
                s_ref[1 - cur, j] = _nt(kb_next, q_chain(j))
            acc_ref[j] = jnp.exp2(m - m_new) * acc_ref[j] + jnp.dot(vtb, p, preferred_element_type=F32)
            m_ref[j] = m_new

    kb0 = k_block(0)
    for j in range(chains):
        s_ref[0, j] = _nt(kb0, q_chain(j))
        m_ref[j] = jnp.full((1, tq), NEG_BIG, F32)
        acc_ref[j] = jnp.zeros((VT_ROWS, tq), F32)

    def body(i, carry):
        step(2 * i, 0, True)
        step(2 * i + 1, 1, True)
        return carry

    lax.fori_loop(0, (nblk - 1) // 2, body, 0)
    if (nblk - 1) % 2:
        step(nblk - 2, 0, True)
    step(nblk - 1, (nblk - 1) % 2, False)
    for j in range(chains):
        o_ref[0, j * tq:(j + 1) * tq, :] = _attn_out(acc_ref[j]).astype(o_ref.dtype)


def _attn_ctx_kernel(q_ref, k_ref, vt_ref, prev_ref, o_ref):
    del prev_ref
    s = _nt(k_ref[0, 0], q_ref[0, 0])
    tq = s.shape[1]
    _, acc = _softmax_block(s, jnp.full((1, tq), NEG_BIG, F32), jnp.zeros((VT_ROWS, tq), F32), vt_ref[0, 0])
    o_ref[0] = _attn_out(acc).astype(o_ref.dtype)


def _attention(q, k, vt, *, n_ctx):
    batch, heads, seq, _ = q.shape
    tile = _pick_tile(seq, (1280, 256))
    tk = _pick_tile(seq, (1280, 256))
    chains = tile // ATT_TQ
    kern = functools.partial(_attn_kernel, chains=chains, tk=tk, seq=seq)
    out_shape = jax.ShapeDtypeStruct((batch, seq, heads * MLA_V), BF16)
    att = pl.pallas_call(
        kern,
        grid=(batch, heads, seq // tile),
        in_specs=[
            pl.BlockSpec((1, 1, tile, HEAD_QK), lambda b, h, i: (b, h, i, 0)),
            pl.BlockSpec((1, 1, seq, HEAD_QK), lambda b, h, i: (b, h, 0, 0)),
            pl.BlockSpec((1, 1, VT_ROWS, seq), lambda b, h, i: (b, h, 0, 0)),
        ],
        out_specs=pl.BlockSpec((1, tile, MLA_V), lambda b, h, i: (b, i, h)),
        out_shape=out_shape,
        scratch_shapes=[pltpu.VMEM((2, chains, tk, ATT_TQ), F32), pltpu.VMEM((chains, 1, ATT_TQ), F32),
                        pltpu.VMEM((chains, VT_ROWS, ATT_TQ), F32)],
        compiler_params=_cparams(("parallel", "parallel", "arbitrary")),
        name="attention",
    )(q, k, vt)
    return pl.pallas_call(
        _attn_ctx_kernel,
        grid=(batch, heads),
        in_specs=[
            pl.BlockSpec((1, 1, n_ctx, HEAD_QK), lambda b, h: (b, h, 0, 0)),
            pl.BlockSpec((1, 1, n_ctx, HEAD_QK), lambda b, h: (b, h, 0, 0)),
            pl.BlockSpec((1, 1, VT_ROWS, n_ctx), lambda b, h: (b, h, 0, 0)),
            pl.BlockSpec(memory_space=pl.ANY),
        ],
        out_specs=pl.BlockSpec((1, n_ctx, MLA_V), lambda b, h: (b, 0, h)),
        out_shape=out_shape,
        input_output_aliases={3: 0},
        compiler_params=_cparams(("parallel", "parallel")),
        name="attention_ctx",
    )(q, k, vt, att)


HALO = BF16_SUBLANES


def _segment_edges(t, tm, tiles, n_ctx):
    start = t * tm
    has_prev = jnp.logical_and(start != 0, start != n_ctx)
    has_next = jnp.logical_and(start + tm != n_ctx, t != tiles - 1)
    return has_prev, has_next


def _conv_kernel(a_ref, g_ref, ap_ref, gp_ref, an_ref, gn_ref, kern_ref, bias_ref, lng_ref, lnb_ref,
                 o_ref, buf_ref, *, tm, tiles, n_ctx):
    t = pl.program_id(1)
    has_prev, has_next = _segment_edges(t, tm, tiles, n_ctx)

    def glu(a, g):
        return a[...].astype(F32) * _sigmoid(g[...].astype(F32))

    buf_ref[0:HALO, :] = jnp.where(has_prev, glu(ap_ref, gp_ref), 0.0)
    buf_ref[HALO:HALO + tm, :] = glu(a_ref, g_ref)
    buf_ref[HALO + tm:, :] = jnp.where(has_next, glu(an_ref, gn_ref), 0.0)

    half = CONV_W // 2
    rb = 32
    for r0 in range(0, tm, rb):
        acc = jnp.zeros((rb, D_MODEL), F32) + bias_ref[...]
        for j in range(CONV_W):
            acc = acc + buf_ref[r0 + HALO - half + j:r0 + HALO - half + j + rb, :] * kern_ref[j:j + 1, :]
        mu = jnp.mean(acc, axis=-1, keepdims=True)
        dlt = acc - mu
        var = jnp.mean(dlt * dlt, axis=-1, keepdims=True)
        y = dlt * lax.rsqrt(var + LN_EPS) * lng_ref[...] + lnb_ref[...]
        o_ref[r0:r0 + rb, :] = (y * _sigmoid(y)).astype(o_ref.dtype)


def _conv(p, kern, bias, lng, lnb, *, batch, seq, n_ctx):
    tm = _pick_tile(n_ctx, (256, 128))
    tiles = seq // tm
    per = tm // HALO
    last_halo = batch * seq // HALO - 1
    ca, cg = COL_GLU_A // D_MODEL, COL_GLU_G // D_MODEL
    cur = lambda c: (lambda b, t: (b * tiles + t, c))
    prev = lambda c: (lambda b, t: (jnp.maximum((b * tiles + t) * per - 1, 0), c))
    nxt = lambda c: (lambda b, t: (jnp.minimum((b * tiles + t + 1) * per, last_halo), c))
    full = lambda shape: pl.BlockSpec(shape, lambda b, t: (0, 0))
    kfn = functools.partial(_conv_kernel, tm=tm, tiles=tiles, n_ctx=n_ctx)
    return pl.pallas_call(
        kfn,
        grid=(batch, tiles),
        in_specs=[
            pl.BlockSpec((tm, D_MODEL), cur(ca)),
            pl.BlockSpec((tm, D_MODEL), cur(cg)),
            pl.BlockSpec((HALO, D_MODEL), prev(ca)),
            pl.BlockSpec((HALO, D_MODEL), prev(cg)),
            pl.BlockSpec((HALO, D_MODEL), nxt(ca)),
            pl.BlockSpec((HALO, D_MODEL), nxt(cg)),
            full(kern.shape), full(bias.shape), full(lng.shape), full(lnb.shape),
        ],
        out_specs=pl.BlockSpec((tm, D_MODEL), lambda b, t: (b * tiles + t, 0)),
        out_shape=jax.ShapeDtypeStruct((batch * seq, D_MODEL), BF16),
        scratch_shapes=[pltpu.VMEM((tm + 2 * HALO, D_MODEL), F32)],
        compiler_params=_cparams(("parallel", "parallel")),
        name="conv",
    )(p, p, p, p, p, p, kern, bias, lng, lnb)


def _head_sum_bcast(x, e_ref, et_ref):
    def two_pass(a, m_ref):
        hi = a.astype(BF16)
        lo = (a - hi.astype(F32)).astype(BF16)
        return (jnp.dot(hi, m_ref[...], preferred_element_type=F32)
                + jnp.dot(lo, m_ref[...], preferred_element_type=F32))

    return two_pass(two_pass(x, e_ref), et_ref)


def _rwkv_prep_kernel(*refs, tm, tiles, n_ctx):
    (r_c, k_c, v_c, l_c, r_p, k_p, v_p, l_p, r_n, k_n, v_n, l_n,
     mu_rkv, mu_lora, w0_ref, a0_ref, w2_ref, a2_ref, g2_ref, kk_ref, ka_ref, rk_ref, e_ref, et_ref,
     r_o, v_o, kk_o, kd_o, b_o, w_o, g_o, bonus_o) = refs
    t = pl.program_id(1)
    has_prev, has_next = _segment_edges(t, tm, tiles, n_ctx)
    row = lax.broadcasted_iota(jnp.int32, (tm, 1), 0)

    def shifted(cur, prv, nxt, mu0, mu1):
        x = cur[...].astype(F32)
        before = jnp.where(has_prev, prv[HALO - 1:HALO, :].astype(F32), 0.0)
        after = jnp.where(has_next, nxt[0:1, :].astype(F32), 0.0)
        x_prev = jnp.where(row == 0, before, pltpu.roll(x, 1, 0))
        x_next = jnp.where(row == tm - 1, after, pltpu.roll(x, tm - 1, 0))
        return x + mu0 * (x_prev - x) + mu1 * (x_next - x)

    def col(ref, i):
        return ref[:, i * D_MODEL:(i + 1) * D_MODEL]

    r = shifted(r_c, r_p, r_n, mu_rkv[0:1, 0:D_MODEL], mu_rkv[1:2, 0:D_MODEL])
    k = shifted(k_c, k_p, k_n, mu_rkv[0:1, D_MODEL:2 * D_MODEL], mu_rkv[1:2, D_MODEL:2 * D_MODEL])
    v = shifted(v_c, v_p, v_n, mu_rkv[0:1, 2 * D_MODEL:], mu_rkv[1:2, 2 * D_MODEL:])
    lo = shifted(l_c, l_p, l_n, mu_lora[0:1, :], mu_lora[1:2, :])
    w1, a1, g1 = lo[:, :LANES], lo[:, LANES:2 * LANES], lo[:, 2 * LANES:]

    lw = jnp.dot(jnp.tanh(w1).astype(BF16), w2_ref[...], preferred_element_type=F32)
    la = jnp.dot(a1.astype(BF16), a2_ref[...], preferred_element_type=F32)
    g = jnp.dot(_sigmoid(g1).astype(BF16), g2_ref[...], preferred_element_type=F32)

    kk = k * kk_ref[...]
    kk = kk * lax.rsqrt(jnp.maximum(_head_sum_bcast(kk * kk, e_ref, et_ref), 1e-24))
    kd_sum = jnp.zeros_like(k)
    for d in range(2):
        x = -(col(w0_ref, d) + col(lw, d))
        softplus = jnp.maximum(x, 0.0) + jnp.log(1.0 + jnp.exp(-jnp.abs(x)))
        w_o[d] = -jnp.exp(-softplus - 0.5)
        a = _sigmoid(col(a0_ref, d) + col(la, d))
        kd = k * (1.0 + (a - 1.0) * ka_ref[...])
        kd_sum = kd_sum + kd
        kd_o[d] = kd.astype(BF16)
        b_o[d] = (kk * a).astype(BF16)
    bonus = _head_sum_bcast(r * kd_sum * rk_ref[...], e_ref, et_ref) * v
    r_o[...] = r.astype(BF16)
    v_o[...] = v.astype(BF16)
    kk_o[...] = kk.astype(BF16)
    g_o[...] = g.astype(BF16)
    bonus_o[...] = bonus.astype(BF16)


def _rwkv_prep(p, mu_rkv, mu_lora, w0, a0, w2, a2, g2, k_k, k_a, r_k, e, et, *, batch, seq, n_ctx):
    tm = _pick_tile(n_ctx, (256, 128))
    tiles = seq // tm
    per = tm // HALO
    m = batch * seq
    last_halo = m // HALO - 1
    cur = lambda c: (lambda b, t: (b * tiles + t, c))
    prev = lambda c: (lambda b, t: (jnp.maximum((b * tiles + t) * per - 1, 0), c))
    nxt = lambda c: (lambda b, t: (jnp.minimum((b * tiles + t + 1) * per, last_halo), c))
    lora_w = 3 * LANES
    cols = [(D_MODEL, COL_R // D_MODEL), (D_MODEL, COL_K // D_MODEL), (D_MODEL, COL_V // D_MODEL),
            (lora_w, COL_LORA // lora_w)]
    in_specs = [pl.BlockSpec((tm, w), cur(c)) for w, c in cols]
    in_specs += [pl.BlockSpec((HALO, w), prev(c)) for w, c in cols]
    in_specs += [pl.BlockSpec((HALO, w), nxt(c)) for w, c in cols]
    consts = [mu_rkv, mu_lora, w0, a0, w2, a2, g2, k_k, k_a, r_k, e, et]
    in_specs += [pl.BlockSpec(c.shape, lambda b, t: (0, 0)) for c in consts]
    row_spec = pl.BlockSpec((tm, D_MODEL), lambda b, t: (b * tiles + t, 0))
    dir_spec = pl.BlockSpec((2, tm, D_MODEL), lambda b, t: (0, b * tiles + t, 0))
    tok = lambda dt: jax.ShapeDtypeStruct((m, D_MODEL), dt)
    dirs = lambda dt: jax.ShapeDtypeStruct((2, m, D_MODEL), dt)
    kfn = functools.partial(_rwkv_prep_kernel, tm=tm, tiles=tiles, n_ctx=n_ctx)
    return pl.pallas_call(
        kfn,
        grid=(batch, tiles),
        in_specs=in_specs,
        out_specs=[row_spec, row_spec, row_spec, dir_spec, dir_spec, dir_spec, row_spec, row_spec],
        out_shape=[tok(BF16), tok(BF16), tok(BF16), dirs(BF16), dirs(BF16), dirs(F32), tok(BF16), tok(BF16)],
        compiler_params=_cparams(("parallel", "parallel")),
        name="rwkv_prep",
    )(*([p] * 12), *consts)


def _nt(a, b):
    return lax.dot_general(a, b, (((1,), (1,)), ((), ())), preferred_element_type=F32)


def _tn(a, b):
    return lax.dot_general(a, b, (((0,), (0,)), ((), ())), preferred_element_type=F32)


def _mm(a, b):
    return jnp.dot(a.astype(BF16), b.astype(BF16), preferred_element_type=F32)


def _scan_kernel(rf_ref, vf_ref, kkf_ref, kdf_ref, bf_ref, wf_ref, rr_ref, vr_ref, kkr_ref, kdr_ref, br_ref, wr_ref,
                 yf_ref, yr_ref, s_ref):
    c_len = SCAN_CHUNK
    n_pairs = D_MODEL // PAIR

    @pl.when(pl.program_id(1) == 0)
    def _():
        s_ref[...] = jnp.zeros_like(s_ref)

    row = lax.broadcasted_iota(jnp.int32, (c_len, 1), 0)
    r2 = lax.broadcasted_iota(jnp.int32, (PAIR, PAIR), 0)
    c2 = lax.broadcasted_iota(jnp.int32, (PAIR, PAIR), 1)
    same = (r2 >> 6) == (c2 >> 6)
    dist = (r2 & (c_len - 1)) - (c2 & (c_len - 1))
    eye = jnp.where(r2 == c2, 1.0, 0.0)
    halves = [jnp.logical_and((r2 >> (l + 1)) == (c2 >> (l + 1)), ((r2 >> l) & 1) != ((c2 >> l) & 1))
              for l in range(6)]

    def stack(x):
        return jnp.where(same, jnp.concatenate([x, x], axis=0), 0.0)

    def dot(a, b):
        return jnp.dot(a, b, preferred_element_type=F32)

    rt, kt, vs, kh, bh, g_tot, sc, strict, incl = [], [], [], [], [], [], [], [], []
    dirs = ((rf_ref, vf_ref, kkf_ref, kdf_ref, bf_ref, wf_ref, 1),
            (rr_ref, vr_ref, kkr_ref, kdr_ref, br_ref, wr_ref, -1))
    for r_ref, v_ref, kk_ref, kd_ref, b_ref, w_ref, sign in dirs:
        w = w_ref[0]
        cum = w
        for step in (1, 2, 4, 8, 16, 32):
            if sign > 0:
                cum = cum + jnp.where(row >= step, pltpu.roll(cum, step, 0), 0.0)
            else:
                cum = cum + jnp.where(row < c_len - step, pltpu.roll(cum, c_len - step, 0), 0.0)
        tot = jnp.sum(w, axis=0, keepdims=True)
        before = jnp.logical_and(same, dist * sign > 0)
        upto = jnp.logical_and(same, dist * sign >= 0)
        for pr in range(n_pairs):
            sl = slice(pr * PAIR, (pr + 1) * PAIR)
            cum_p, w_p, tot_p = cum[:, sl], w[:, sl], tot[:, sl]
            g_inv = jnp.exp(-cum_p)
            g_tail = jnp.exp(tot_p - cum_p)
            kd = kd_ref[0, :, sl].astype(F32)
            b = b_ref[0, :, sl].astype(F32)
            rt.append(stack(r_ref[:, sl].astype(F32) * jnp.exp(cum_p)))
            kt.append(stack(kk_ref[:, sl].astype(F32) * jnp.exp(cum_p - w_p)).astype(BF16))
            vs.append(stack(v_ref[:, sl].astype(F32)).astype(BF16))
            kh.append(stack(kd * g_tail).astype(BF16))
            bh.append(stack(b * g_tail).astype(BF16))
            g_tot.append(jnp.exp(tot_p))
            strict.append(before)
            incl.append(upto)
            lhs = jnp.concatenate([kt[-1], rt[-1].astype(BF16)], axis=0)
            rhs = jnp.concatenate([stack(kd * g_inv), stack(b * g_inv)], axis=0).astype(BF16)
            sc.append(_nt(lhs, rhs))

    probs = range(2 * n_pairs)
    a_b = [jnp.where(incl[q], sc[q][PAIR:, PAIR:], 0.0).astype(BF16) for q in probs]
    mkv = [dot(jnp.where(strict[q], sc[q][:PAIR, :PAIR], 0.0).astype(BF16), vs[q]) for q in probs]
    akv = [dot(jnp.where(incl[q], sc[q][PAIR:, :PAIR], 0.0).astype(BF16), vs[q]) for q in probs]
    vtk = [_tn(vs[q], kh[q]) for q in probs]

    m_b = [jnp.where(strict[q], sc[q][:PAIR, PAIR:], 0.0) for q in probs]
    t_inv = [eye - jnp.where(halves[0], m_b[q], 0.0) for q in probs]
    for lvl in range(1, len(halves)):
        t_bf = [t_inv[q].astype(BF16) for q in probs]
        dx = [dot(t_bf[q], jnp.where(halves[lvl], m_b[q], 0.0).astype(BF16)).astype(BF16) for q in probs]
        t_inv = [t_inv[q] - dot(dx[q], t_bf[q]) for q in probs]
    t_inv = [t_inv[q].astype(BF16) for q in probs]

    wu = [dot(t_inv[q], jnp.concatenate([kt[q], mkv[q].astype(BF16)], axis=1)).astype(BF16)
          for q in probs]
    wub = [_tn(wu[q], bh[q]) for q in probs]
    abwu = [dot(a_b[q], wu[q]) for q in probs]
    s0 = [s_ref[q] for q in probs]
    s0b = [s0[q].astype(BF16) for q in probs]
    y = [_nt((rt[q] - abwu[q][:, :PAIR]).astype(BF16), s0b[q]) for q in probs]
    ds = [dot(s0b[q], wub[q][:PAIR].astype(BF16)) for q in probs]
    for q in probs:
        sl = slice((q % n_pairs) * PAIR, (q % n_pairs + 1) * PAIR)
        s_ref[q] = s0[q] * g_tot[q] + (vtk[q] - wub[q][PAIR:] - ds[q])
        y_st = y[q] + akv[q] - abwu[q][:, PAIR:]
        y_ref = yf_ref if q < n_pairs else yr_ref
        y_ref[:, sl] = y_st[:c_len] + y_st[c_len:]


def _rwkv_scan(r, v, kk, kd, b, w, *, batch, seq, n_ctx):
    c_len = SCAN_CHUNK
    n_c = seq // c_len
    n_cc = n_ctx // c_len
    m = batch * seq

    def fwd(bi, c):
        return bi * n_c + c

    def rev(bi, c):
        return bi * n_c + jnp.where(c < n_cc, n_cc - 1 - c, n_c - 1 - (c - n_cc))

    def specs(blk, d):
        tok = pl.BlockSpec((c_len, D_MODEL), lambda bi, c: (blk(bi, c), 0))
        per_dir = pl.BlockSpec((1, c_len, D_MODEL), lambda bi, c: (d, blk(bi, c), 0))
        return [tok, tok, tok, per_dir, per_dir, per_dir]

    out = jax.ShapeDtypeStruct((m, D_MODEL), F32)
    return pl.pallas_call(
        _scan_kernel,
        grid=(batch, n_c),
        in_specs=specs(fwd, 0) + specs(rev, 1),
        out_specs=[pl.BlockSpec((c_len, D_MODEL), lambda bi, c: (fwd(bi, c), 0)),
                   pl.BlockSpec((c_len, D_MODEL), lambda bi, c: (rev(bi, c), 0))],
        out_shape=[out, out],
        scratch_shapes=[pltpu.VMEM((2 * D_MODEL // PAIR, PAIR, PAIR), F32)],
        compiler_params=_cparams(("parallel", "arbitrary")),
        name="rwkv_scan",
    )(r, v, kk, kd, b, w, r, v, kk, kd, b, w)


def _mix_kernel(z_ref, mod_ref, att_ref, cv_ref, y0_ref, y1_ref, g_ref, bonus_ref, gates_ref,
                woa_ref, wpw_ref, wor_ref, wout_ref, lng_ref, lnb_ref, e_ref, et_ref, o_ref,
                *, tm, tiles, n_ctx, ctx_row):
    i = pl.program_id(0)
    y_a = jnp.dot(att_ref[...], woa_ref[...], preferred_element_type=F32)
    y_b = jnp.dot(cv_ref[...], wpw_ref[...], preferred_element_type=F32)

    y = y0_ref[...] + y1_ref[...]
    inv_n = 1.0 / RWKV_HEAD
    mu = _head_sum_bcast(y, e_ref, et_ref) * inv_n
    dlt = y - mu
    var = _head_sum_bcast(dlt * dlt, e_ref, et_ref) * inv_n
    yn = dlt * lax.rsqrt(var + GN_EPS) * lng_ref[...] + lnb_ref[...]
    rw = (yn + bonus_ref[...].astype(F32)) * g_ref[...].astype(F32)
    y_c = jnp.dot(rw.astype(BF16), wor_ref[...], preferred_element_type=F32)

    def gate(j):
        return _sigmoid(gates_ref[:, j * D_MODEL:(j + 1) * D_MODEL].astype(F32))

    mix = gate(0) * y_a + gate(1) * y_b + gate(2) * y_c
    out = jnp.dot(mix.astype(BF16), wout_ref[...], preferred_element_type=F32)
    g_msa = _mod_rows(mod_ref, 2, i // tiles, (i % tiles) * tm, tm, n_ctx, ctx_row)
    o_ref[...] = z_ref[...] + g_msa * out


def _mix_out(z, mod, att, cv, y_fwd, y_rev, g, bonus, p, woa, wpw, wor, wout, lng, lnb, e, et, *, seq, n_ctx, ctx_row):
    m, d = z.shape
    tm = _pick_tile(seq, (320, 256, 128))
    tiles = seq // tm
    row = pl.BlockSpec((tm, d), lambda i: (i, 0))
    full = lambda a: pl.BlockSpec(a.shape, lambda i: (0,) * a.ndim)
    kfn = functools.partial(_mix_kernel, tm=tm, tiles=tiles, n_ctx=n_ctx, ctx_row=ctx_row)
    return pl.pallas_call(
        kfn,
        grid=(m // tm,),
        in_specs=[
            row, full(mod), row, row,
            row, row, row, row,
            pl.BlockSpec((tm, 3 * d), lambda i: (i, COL_GATES // (3 * d))),
            full(woa), full(wpw), full(wor), full(wout), full(lng), full(lnb), full(e), full(et),
        ],
        out_specs=row,
        out_shape=jax.ShapeDtypeStruct((m, d), F32),
        compiler_params=_cparams(("parallel",)),
        name="mix_out",
    )(z, mod, att, cv, y_fwd, y_rev, g, bonus, p, woa, wpw, wor, wout, lng, lnb, e, et)


def _mlp_kernel(z_ref, mod_ref, g_ref, wu_ref, wd_ref, fg_ref, o_ref, h_ref, acc_ref,
                *, tm, tiles, n_ctx, ctx_row, final):
    i = pl.program_id(0)
    f = pl.program_id(1)
    batch, row0 = i // tiles, (i % tiles) * tm

    @pl.when(f == 0)
    def _():
        h = _adaln(z_ref[...], g_ref[...], mod_ref, 3, 4, batch, row0, tm, n_ctx, ctx_row)
        h_ref[...] = h.astype(BF16)
        acc_ref[...] = jnp.zeros_like(acc_ref)

    a = jnp.dot(h_ref[...], wu_ref[...], preferred_element_type=F32)
    a = jnp.square(jnp.maximum(a, 0.0))
    acc_ref[...] += jnp.dot(a.astype(BF16), wd_ref[...], preferred_element_type=F32)

    @pl.when(f == pl.num_programs(1) - 1)
    def _():
        out = z_ref[...] + _mod_rows(mod_ref, 5, batch, row0, tm, n_ctx, ctx_row) * acc_ref[...]
        if final:
            out = _rms(out, fg_ref[...])
        o_ref[...] = out


def _mlp(z, mod, g, wu, wd, fg, *, seq, n_ctx, ctx_row, final):
    m, d = z.shape
    ff = wu.shape[1]
    tm = _pick_tile(seq, (1280, 640, 320, 256))
    tf = 512
    kfn = functools.partial(_mlp_kernel, tm=tm, tiles=seq // tm, n_ctx=n_ctx, ctx_row=ctx_row, final=final)
    return pl.pallas_call(
        kfn,
        grid=(m // tm, ff // tf),
        in_specs=[
            pl.BlockSpec((tm, d), lambda i, f: (i, 0)),
            pl.BlockSpec(mod.shape, lambda i, f: (0, 0, 0)),
            pl.BlockSpec((1, d), lambda i, f: (0, 0)),
            pl.BlockSpec((d, tf), lambda i, f: (0, f)),
            pl.BlockSpec((tf, d), lambda i, f: (f, 0)),
            pl.BlockSpec((1, d), lambda i, f: (0, 0)),
        ],
        out_specs=pl.BlockSpec((tm, d), lambda i, f: (i, 0)),
        out_shape=jax.ShapeDtypeStruct((m, d), F32),
        scratch_shapes=[pltpu.VMEM((tm, d), BF16), pltpu.VMEM((tm, d), F32)],
        compiler_params=_cparams(("parallel", "arbitrary")),
        name="mlp",
    )(z, mod, g, wu, wd, fg)


def _layout_w_in(w_in):
    depth = w_in.shape[0]
    off_q = Q_LORA
    off_kv = off_q + KV_LORA + MLA_ROPE
    off_glu = off_kv + 2 * D_MODEL
    rw = off_glu
    rw_in = 3 * D_MODEL + 2 * DECAY_LORA + 2 * ICL_LORA + GATE_LORA
    off_rwkv = off_glu + rw_in
    seg = lambda a, b: w_in[:, :, a:b]
    k_rope = seg(Q_LORA + KV_LORA, off_kv)
    swap = jnp.arange(MLA_ROPE) ^ ROPE_FREQS
    z64 = jnp.zeros(w_in.shape[:2] + (LANES - MLA_ROPE,), w_in.dtype)
    parts = [
        seg(off_rwkv, off_rwkv + 3 * D_MODEL),
        seg(off_kv, off_kv + 2 * D_MODEL),
        seg(rw, rw + 3 * D_MODEL),
        seg(0, Q_LORA), seg(Q_LORA, Q_LORA + KV_LORA),
        k_rope, z64, k_rope[:, :, swap], z64,
        seg(rw + 3 * D_MODEL, rw + rw_in),
    ]
    w = jnp.concatenate(parts, axis=-1)
    pad = D_IN_PAD - w.shape[-1]
    return jnp.pad(w, ((0, 0), (0, 0), (0, pad))).astype(BF16)


def _layout_w_q(w_q_b):
    depth = w_q_b.shape[0]
    w = w_q_b.reshape(depth, Q_LORA, MLA_HEADS, MLA_NOPE + MLA_ROPE)
    nope, rope = w[..., :MLA_NOPE], w[..., MLA_NOPE:]
    swap = jnp.arange(MLA_ROPE) ^ ROPE_FREQS
    z64 = jnp.zeros(rope.shape[:-1] + (LANES - MLA_ROPE,), w.dtype)
    w = jnp.concatenate([nope, rope, z64, rope[..., swap], z64], axis=-1)
    return w.reshape(depth, Q_LORA, MLA_HEADS * 3 * LANES).astype(BF16)


def _rope_tables(n_ctx, n_lat):
    rows = n_lat // GRID_W
    row = jnp.repeat(jnp.arange(rows), GRID_W).astype(F32)
    colp = jnp.tile(jnp.arange(GRID_W), rows).astype(F32)
    inv_freq = ROPE_THETA ** (-jnp.arange(ROPE_FREQS, dtype=F32) / ROPE_FREQS)
    ang_lat = jnp.stack([row[:, None] * inv_freq, colp[:, None] * inv_freq], axis=1)
    ang = jnp.concatenate([jnp.zeros((n_ctx, 2, ROPE_FREQS), F32), ang_lat], axis=0)
    cos, sin = jnp.cos(ang), jnp.sin(ang)
    cos_t = jnp.stack([cos, cos], axis=2).reshape(-1, MLA_ROPE)
    sin_t = jnp.stack([-sin, sin], axis=2).reshape(-1, MLA_ROPE)
    pad = ((0, 0), (0, LANES - MLA_ROPE))
    return jnp.pad(cos_t, pad), jnp.pad(sin_t, pad)


def _block_diag2(w):
    z = jnp.zeros_like(w[0])
    return jnp.concatenate([jnp.concatenate([w[0], z], axis=1), jnp.concatenate([z, w[1]], axis=1)], axis=0)


def kernel(x, c, ctx, c_ctx, w_mod, b_mod, norm1_g, norm2_g, w_in, q_a_norm_g, w_q_b, kv_a_norm_g, w_kv_b, w_o_mla, dw_kernel, dw_bias, conv_ln_g, conv_ln_b, w_pw2, rwkv_mu, rwkv_w0, rwkv_w2, rwkv_a0, rwkv_a2, rwkv_g2, rwkv_k_k, rwkv_k_a, rwkv_r_k, rwkv_ln_g, rwkv_ln_b, w_o_rwkv, w_out, w_up, w_down, final_norm_g):
    batch, n_lat, d = x.shape
    n_ctx = ctx.shape[1]
    seq = n_ctx + n_lat
    depth = w_mod.shape[0]
    assert d == D_MODEL and batch < 8 and n_ctx % SCAN_CHUNK == 0 and seq % SCAN_CHUNK == 0
    ctx_row = batch

    xc = jnp.zeros((8, d), F32).at[:batch].set(c).at[batch].set(c_ctx)
    mod_all = _modulation(xc, w_mod, b_mod).reshape(depth, 8, 6, d).transpose(0, 2, 1, 3)

    w_in_l = _layout_w_in(w_in)
    w_q_l = _layout_w_q(w_q_b)
    w_kv_h = w_kv_b.reshape(depth, KV_LORA, MLA_HEADS, MLA_NOPE + MLA_V)
    w_k_l = w_kv_h[..., :MLA_NOPE].reshape(depth, KV_LORA, MLA_HEADS * MLA_NOPE).astype(BF16)
    w_vt_l = w_kv_h[..., MLA_NOPE:].transpose(0, 2, 3, 1).reshape(depth, MLA_HEADS * MLA_V, KV_LORA).astype(BF16)
    cos_t, sin_t = _rope_tables(n_ctx, n_lat)
    head_of = jnp.arange(D_MODEL) // RWKV_HEAD
    e = (head_of[:, None] == jnp.arange(LANES)[None, :]).astype(BF16)
    et = e.T
    rw_cols = 3 * D_MODEL
    row2 = lambda a: a.reshape(1, -1)
    kern_pad = jnp.pad(dw_kernel, ((0, 0), (0, 32 - CONV_W), (0, 0)))

    z = jnp.concatenate([ctx, x], axis=1).reshape(batch * seq, d)
    for l in range(depth):
        mod = mod_all[l]
        p = _in_proj(z, mod, row2(norm1_g[l]), w_in_l[l], seq=seq, n_ctx=n_ctx, ctx_row=ctx_row)

        q, k, vt = _qkv(p, cos_t, sin_t, row2(q_a_norm_g[l]), row2(kv_a_norm_g[l]), w_q_l[l], w_k_l[l], w_vt_l[l],
                        batch=batch, seq=seq)
        att = _attention(q, k, vt, n_ctx=n_ctx).reshape(batch * seq, MLA_HEADS * MLA_V)

        cv = _conv(p, kern_pad[l], row2(dw_bias[l]), row2(conv_ln_g[l]), row2(conv_ln_b[l]),
                   batch=batch, seq=seq, n_ctx=n_ctx)

        r_s, v_s, kk_s, kd_s, b_s, w_s, g_s, bonus = _rwkv_prep(
            p, rwkv_mu[l][:, :rw_cols], rwkv_mu[l][:, rw_cols:],
            rwkv_w0[l].reshape(1, 2 * d), rwkv_a0[l].reshape(1, 2 * d),
            _block_diag2(rwkv_w2[l]).astype(BF16), _block_diag2(rwkv_a2[l]).astype(BF16),
            rwkv_g2[l].astype(BF16), row2(rwkv_k_k[l]), row2(rwkv_k_a[l]), rwkv_r_k[l].reshape(1, d),
            e, et, batch=batch, seq=seq, n_ctx=n_ctx)
        y_fwd, y_rev = _rwkv_scan(r_s, v_s, kk_s, kd_s, b_s, w_s, batch=batch, seq=seq, n_ctx=n_ctx)

        z = _mix_out(z, mod, att, cv, y_fwd, y_rev, g_s, bonus, p,
                     w_o_mla[l].astype(BF16), w_pw2[l].astype(BF16), w_o_rwkv[l].astype(BF16),
                     w_out[l].astype(BF16), row2(rwkv_ln_g[l]), row2(rwkv_ln_b[l]), e, et,
                     seq=seq, n_ctx=n_ctx, ctx_row=ctx_row)

        z = _mlp(z, mod, row2(norm2_g[l]), w_up[l].astype(BF16), w_down[l].astype(BF16),
                 row2(final_norm_g), seq=seq, n_ctx=n_ctx, ctx_row=ctx_row, final=(l == depth - 1))

    return z.reshape(batch, seq, d)[:, n_ctx:]
```

```python
import functools
import math

import jax
import jax.numpy as jnp
from jax import lax
from jax.experimental import pallas as pl
from jax.experimental.pallas import tpu as pltpu

F32 = jnp.float32
BF16 = jnp.bfloat16
HIGHEST = lax.Precision.HIGHEST

D_MODEL = 1024
GRID_W = 64
NORM_EPS = 1e-6
LN_EPS = 1e-5
MLA_HEADS = 8
MLA_NOPE = 128
MLA_ROPE = 64
MLA_V = 128
Q_LORA = 512
KV_LORA = 256
ROPE_FREQS = MLA_ROPE // 4
ROPE_THETA = 10000.0
CONV_W = 31
RWKV_HEAD = 64
RWKV_HEADS = D_MODEL // RWKV_HEAD
DECAY_LORA = 64
ICL_LORA = 64
GATE_LORA = 128
GN_EPS = 64e-5
D_FF = 4 * D_MODEL

LANES = 128
SUBLANES = 8
BF16_SUBLANES = 16
VMEM_LIMIT = 56 * 1024 * 1024

COL_GATES = 0
COL_GLU_A = 3 * D_MODEL
COL_GLU_G = 4 * D_MODEL
COL_R = 5 * D_MODEL
COL_K = 6 * D_MODEL
COL_V = 7 * D_MODEL
COL_QA = 8 * D_MODEL
COL_KVL = COL_QA + Q_LORA
COL_KR = COL_KVL + KV_LORA
COL_LORA = COL_KR + 2 * LANES
D_IN_PAD = 19 * 512
HEAD_QK = 2 * LANES

SCAN_CHUNK = 64
PAIR = 2 * RWKV_HEAD


def _cparams(sem):
    return pltpu.CompilerParams(dimension_semantics=sem, vmem_limit_bytes=VMEM_LIMIT)


def _pick_tile(n, prefs):
    for t in prefs:
        if n % t == 0:
            return t
    raise ValueError(f"no tile in {prefs} divides {n}")


def _sigmoid(x):
    return 0.5 * jnp.tanh(0.5 * x) + 0.5


def _mod_kernel(x_ref, w_ref, b_ref, o_ref):
    x = x_ref[...]
    s = x * _sigmoid(x)
    o_ref[0] = jnp.dot(s, w_ref[0], precision=HIGHEST, preferred_element_type=F32) + b_ref[0]


def _modulation(xc, w_mod, b_mod):
    depth, d, n = w_mod.shape
    tn = _pick_tile(n, (1536, 1024, 512, 128))
    return pl.pallas_call(
        _mod_kernel,
        grid=(depth, n // tn),
        in_specs=[
            pl.BlockSpec((8, d), lambda l, j: (0, 0)),
            pl.BlockSpec((1, d, tn), lambda l, j: (l, 0, j)),
            pl.BlockSpec((1, 1, tn), lambda l, j: (l, 0, j)),
        ],
        out_specs=pl.BlockSpec((1, 8, tn), lambda l, j: (l, 0, j)),
        out_shape=jax.ShapeDtypeStruct((depth, 8, n), F32),
        compiler_params=_cparams(("parallel", "parallel")),
        name="modulation",
    )(xc, w_mod, b_mod.reshape(depth, 1, n))


def _mod_rows(mod_ref, slot, batch, row0, tm, n_ctx, ctx_row):
    rows = row0 + lax.broadcasted_iota(jnp.int32, (tm, 1), 0)
    lat = mod_ref[slot, pl.ds(batch, 1), :]
    ctx = mod_ref[slot, pl.ds(ctx_row, 1), :]
    return jnp.where(rows < n_ctx, ctx, lat)


def _rms(x, g):
    return x * lax.rsqrt(jnp.mean(x * x, axis=-1, keepdims=True) + NORM_EPS) * g


def _adaln(z, g, mod_ref, s_shift, s_scale, batch, row0, tm, n_ctx, ctx_row):
    shift = _mod_rows(mod_ref, s_shift, batch, row0, tm, n_ctx, ctx_row)
    scale = _mod_rows(mod_ref, s_scale, batch, row0, tm, n_ctx, ctx_row)
    return _rms(z, g) * (1.0 + scale) + shift


def _in_proj_kernel(z_ref, mod_ref, g_ref, w_ref, o_ref, h_ref, *, tm, tiles, n_ctx, ctx_row):
    i = pl.program_id(0)

    @pl.when(pl.program_id(1) == 0)
    def _():
        h = _adaln(z_ref[...], g_ref[...], mod_ref, 0, 1, i // tiles, (i % tiles) * tm, tm, n_ctx, ctx_row)
        h_ref[...] = h.astype(BF16)

    o_ref[...] = jnp.dot(h_ref[...], w_ref[...], preferred_element_type=F32).astype(o_ref.dtype)


def _in_proj(z, mod, g, w, *, seq, n_ctx, ctx_row):
    m, d = z.shape
    n = w.shape[1]
    tm = _pick_tile(seq, (1664, 1280, 640, 320, 256))
    tn = 512
    kern = functools.partial(_in_proj_kernel, tm=tm, tiles=seq // tm, n_ctx=n_ctx, ctx_row=ctx_row)
    return pl.pallas_call(
        kern,
        grid=(m // tm, n // tn),
        in_specs=[
            pl.BlockSpec((tm, d), lambda i, j: (i, 0)),
            pl.BlockSpec(mod.shape, lambda i, j: (0, 0, 0)),
            pl.BlockSpec((1, d), lambda i, j: (0, 0)),
            pl.BlockSpec((d, tn), lambda i, j: (0, j)),
        ],
        out_specs=pl.BlockSpec((tm, tn), lambda i, j: (i, j)),
        out_shape=jax.ShapeDtypeStruct((m, n), BF16),
        scratch_shapes=[pltpu.VMEM((tm, d), BF16)],
        compiler_params=_cparams(("parallel", "arbitrary")),
        name="in_proj",
    )(z, mod, g, w)


VT_ROWS = MLA_V + BF16_SUBLANES


def _qkv_kernel(qa_ref, kvl_ref, kr_ref, cos_ref, sin_ref, qg_ref, kvg_ref, wq_ref, wk_ref, wvt_ref,
                q_ref, k_ref, vt_ref, *, q_scale):
    cos = cos_ref[...]
    sin = sin_ref[...]
    qn = _rms(qa_ref[...].astype(F32), qg_ref[...]).astype(BF16)
    kvn = _rms(kvl_ref[...].astype(F32), kvg_ref[...]).astype(BF16)
    kr = kr_ref[...].astype(F32)
    k_rope = (kr[:, :LANES] * cos + kr[:, LANES:] * sin).astype(BF16)
    tm = qn.shape[0]
    ones_rows = jnp.where(lax.broadcasted_iota(jnp.int32, (BF16_SUBLANES, tm), 0) == 0, 1.0, 0.0).astype(BF16)
    for h in range(MLA_HEADS):
        qh = jnp.dot(qn, wq_ref[:, h * 3 * LANES:(h + 1) * 3 * LANES], preferred_element_type=F32)
        rope = qh[:, LANES:2 * LANES] * cos + qh[:, 2 * LANES:] * sin
        q_ref[0, h, :, :LANES] = (qh[:, :LANES] * q_scale).astype(BF16)
        q_ref[0, h, :, LANES:] = (rope * q_scale).astype(BF16)
        kh = jnp.dot(kvn, wk_ref[:, h * MLA_NOPE:(h + 1) * MLA_NOPE], preferred_element_type=F32)
        k_ref[0, h, :, :LANES] = kh.astype(BF16)
        k_ref[0, h, :, LANES:] = k_rope
        vt = _nt(wvt_ref[h * MLA_V:(h + 1) * MLA_V, :], kvn)
        vt_ref[0, h, :MLA_V, :] = vt.astype(BF16)
        vt_ref[0, h, MLA_V:, :] = ones_rows


def _qkv(p, cos_t, sin_t, qg, kvg, wq, wk, wvt, *, batch, seq):
    tm = _pick_tile(seq, (640, 256))
    tiles = seq // tm
    q_scale = (MLA_NOPE + MLA_ROPE) ** -0.5 * math.log2(math.e)
    kern = functools.partial(_qkv_kernel, q_scale=q_scale)
    row = lambda b, t: b * tiles + t
    return pl.pallas_call(
        kern,
        grid=(batch, tiles),
        in_specs=[
            pl.BlockSpec((tm, Q_LORA), lambda b, t: (row(b, t), COL_QA // Q_LORA)),
            pl.BlockSpec((tm, KV_LORA), lambda b, t: (row(b, t), COL_KVL // KV_LORA)),
            pl.BlockSpec((tm, 2 * LANES), lambda b, t: (row(b, t), COL_KR // (2 * LANES))),
            pl.BlockSpec((tm, LANES), lambda b, t: (t, 0)),
            pl.BlockSpec((tm, LANES), lambda b, t: (t, 0)),
            pl.BlockSpec((1, Q_LORA), lambda b, t: (0, 0)),
            pl.BlockSpec((1, KV_LORA), lambda b, t: (0, 0)),
            pl.BlockSpec(wq.shape, lambda b, t: (0, 0)),
            pl.BlockSpec(wk.shape, lambda b, t: (0, 0)),
            pl.BlockSpec(wvt.shape, lambda b, t: (0, 0)),
        ],
        out_specs=[
            pl.BlockSpec((1, MLA_HEADS, tm, HEAD_QK), lambda b, t: (b, 0, t, 0)),
            pl.BlockSpec((1, MLA_HEADS, tm, HEAD_QK), lambda b, t: (b, 0, t, 0)),
            pl.BlockSpec((1, MLA_HEADS, VT_ROWS, tm), lambda b, t: (b, 0, 0, t)),
        ],
        out_shape=[
            jax.ShapeDtypeStruct((batch, MLA_HEADS, seq, HEAD_QK), BF16),
            jax.ShapeDtypeStruct((batch, MLA_HEADS, seq, HEAD_QK), BF16),
            jax.ShapeDtypeStruct((batch, MLA_HEADS, VT_ROWS, seq), BF16),
        ],
        compiler_params=_cparams(("parallel", "parallel")),
        name="qkv",
    )(p, p, p, cos_t, sin_t, qg, kvg, wq, wk, wvt)


ATT_TQ = 256
NEG_BIG = -1e30


def _softmax_block(s, m, acc, vtb):
    m_new = jnp.maximum(m, jnp.max(s, axis=0, keepdims=True))
    alpha = jnp.exp2(m - m_new)
    p = jnp.exp2(s - m_new).astype(BF16)
    return m_new, alpha * acc + jnp.dot(vtb, p, preferred_element_type=F32)


def _attn_out(acc):
    return (acc[:MLA_V] / acc[MLA_V:MLA_V + 1]).T


def _attn_kernel(q_ref, k_ref, vt_ref, o_ref, s_ref, m_ref, acc_ref, *, chains, tk, seq):
    tq = ATT_TQ
    nblk = seq // tk

    def q_chain(j):
        return q_ref[0, 0, j * tq:(j + 1) * tq, :]

    def k_block(c):
        return k_ref[0, 0, pl.ds(pl.multiple_of(c * tk, tk), tk), :]

    def step(c, cur, prefetch):
        vtb = vt_ref[0, 0, :, pl.ds(pl.multiple_of(c * tk, tk), tk)]
        kb_next = k_block(c + 1) if prefetch else None
        for j in range(chains):
            s = s_ref[cur, j]
            m = m_ref[j]
            m_new = jnp.maximum(m, jnp.max(s, axis=0, keepdims=True))
            p = jnp.exp2(s - m_new).astype(BF16)
            if prefetch:
                s_ref[1 - cur, j] = _nt(kb_next, q_chain(j))
            acc_ref[j] = jnp.exp2(m - m_new) * acc_ref[j] + jnp.dot(vtb, p, preferred_element_type=F32)
            m_ref[j] = m_new

    kb0 = k_block(0)
    for j in range(chains):
        s_ref[0, j] = _nt(kb0, q_chain(j))
        m_ref[j] = jnp.full((1, tq), NEG_BIG, F32)
        acc_ref[j] = jnp.zeros((VT_ROWS, tq), F32)

    def body(i, carry):
        step(2 * i, 0, True)
        step(2 * i + 1, 1, True)
        return carry

    lax.fori_loop(0, (nblk - 1) // 2, body, 0)
    if (nblk - 1) % 2:
        step(nblk - 2, 0, True)
    step(nblk - 1, (nblk - 1) % 2, False)
    for j in range(chains):
        o_ref[0, j * tq:(j + 1) * tq, :] = _attn_out(acc_ref[j]).astype(o_ref.dtype)


def _attn_ctx_kernel(q_ref, k_ref, vt_ref, prev_ref, o_ref):
    del prev_ref
    s = _nt(k_ref[0, 0], q_ref[0, 0])
    tq = s.shape[1]
    _, acc = _softmax_block(s, jnp.full((1, tq), NEG_BIG, F32), jnp.zeros((VT_ROWS, tq), F32), vt_ref[0, 0])
    o_ref[0] = _attn_out(acc).astype(o_ref.dtype)


def _attention(q, k, vt, *, n_ctx):
    batch, heads, seq, _ = q.shape
    tile = _pick_tile(seq, (1280, 256))
    tk = _pick_tile(seq, (1280, 256))
    chains = tile // ATT_TQ
    kern = functools.partial(_attn_kernel, chains=chains, tk=tk, seq=seq)
    out_shape = jax.ShapeDtypeStruct((batch, seq, heads * MLA_V), BF16)
    att = pl.pallas_call(
        kern,
        grid=(batch, heads, seq // tile),
        in_specs=[
            pl.BlockSpec((1, 1, tile, HEAD_QK), lambda b, h, i: (b, h, i, 0)),
            pl.BlockSpec((1, 1, seq, HEAD_QK), lambda b, h, i: (b, h, 0, 0)),
            pl.BlockSpec((1, 1, VT_ROWS, seq), lambda b, h, i: (b, h, 0, 0)),
        ],
        out_specs=pl.BlockSpec((1, tile, MLA_V), lambda b, h, i: (b, i, h)),
        out_shape=out_shape,
        scratch_shapes=[pltpu.VMEM((2, chains, tk, ATT_TQ), F32), pltpu.VMEM((chains, 1, ATT_TQ), F32),
                        pltpu.VMEM((chains, VT_ROWS, ATT_TQ), F32)],
        compiler_params=_cparams(("parallel", "parallel", "arbitrary")),
        name="attention",
    )(q, k, vt)
    return pl.pallas_call(
        _attn_ctx_kernel,
        grid=(batch, heads),
        in_specs=[
            pl.BlockSpec((1, 1, n_ctx, HEAD_QK), lambda b, h: (b, h, 0, 0)),
            pl.BlockSpec((1, 1, n_ctx, HEAD_QK), lambda b, h: (b, h, 0, 0)),
            pl.BlockSpec((1, 1, VT_ROWS, n_ctx), lambda b, h: (b, h, 0, 0)),
            pl.BlockSpec(memory_space=pl.ANY),
        ],
        out_specs=pl.BlockSpec((1, n_ctx, MLA_V), lambda b, h: (b, 0, h)),
        out_shape=out_shape,
        input_output_aliases={3: 0},
        compiler_params=_cparams(("parallel", "parallel")),
        name="attention_ctx",
    )(q, k, vt, att)


HALO = BF16_SUBLANES
CONV_ROWS = 64


def _segment_edges(t, tm, tiles, n_ctx):
    start = t * tm
    has_prev = jnp.logical_and(start != 0, start != n_ctx)
    has_next = jnp.logical_and(start + tm != n_ctx, t != tiles - 1)
    return has_prev, has_next


def _conv_kernel(a_ref, g_ref, ap_ref, gp_ref, an_ref, gn_ref, kern_ref, bias_ref, lng_ref, lnb_ref,
                 o_ref, buf_ref, part_ref, *, tm, tiles, n_ctx):
    t = pl.program_id(1)
    has_prev, has_next = _segment_edges(t, tm, tiles, n_ctx)

    def glu(a, g):
        return a[...].astype(F32) * _sigmoid(g[...].astype(F32))

    buf_ref[0:HALO, :] = jnp.where(has_prev, glu(ap_ref, gp_ref), 0.0)
    buf_ref[HALO:HALO + tm, :] = glu(a_ref, g_ref)
    buf_ref[HALO + tm:, :] = jnp.where(has_next, glu(an_ref, gn_ref), 0.0)

    assert HALO - CONV_W // 2 == 1
    rb = CONV_ROWS
    sub = SUBLANES
    n_a = (CONV_W + sub) // sub
    for r0 in range(0, tm, rb):
        for lg in range(D_MODEL // LANES):
            ls = slice(lg * LANES, (lg + 1) * LANES)
            xs = [buf_ref[r0 + sub * a:r0 + sub * a + rb + sub, ls] for a in range(n_a)]
            for b in range(sub):
                part = None
                for a in range(n_a):
                    u = sub * a + b
                    if 1 <= u <= CONV_W:
                        term = xs[a] * pltpu.repeat(kern_ref[u - 1, :, ls], (rb + sub) // sub, 0)
                        part = term if part is None else part + term
                part_ref[b, :, ls] = part
        acc = bias_ref[...] + part_ref[0, 0:rb, :]
        for b in range(1, sub):
            acc = acc + part_ref[b, b:b + rb, :]
        mu = jnp.mean(acc, axis=-1, keepdims=True)
        dlt = acc - mu
        var = jnp.mean(dlt * dlt, axis=-1, keepdims=True)
        y = dlt * lax.rsqrt(var + LN_EPS) * lng_ref[...] + lnb_ref[...]
        o_ref[r0:r0 + rb, :] = (y * _sigmoid(y)).astype(o_ref.dtype)


def _conv(p, kern, bias, lng, lnb, *, batch, seq, n_ctx):
    tm = _pick_tile(n_ctx, (256, 128))
    tiles = seq // tm
    per = tm // HALO
    last_halo = batch * seq // HALO - 1
    ca, cg = COL_GLU_A // D_MODEL, COL_GLU_G // D_MODEL
    cur = lambda c: (lambda b, t: (b * tiles + t, c))
    prev = lambda c: (lambda b, t: (jnp.maximum((b * tiles + t) * per - 1, 0), c))
    nxt = lambda c: (lambda b, t: (jnp.minimum((b * tiles + t + 1) * per, last_halo), c))
    full = lambda shape: pl.BlockSpec(shape, lambda b, t: (0,) * len(shape))
    kern = jnp.broadcast_to(kern[:, None, :], (CONV_W, SUBLANES, D_MODEL))
    kfn = functools.partial(_conv_kernel, tm=tm, tiles=tiles, n_ctx=n_ctx)
    return pl.pallas_call(
        kfn,
        grid=(batch, tiles),
        in_specs=[
            pl.BlockSpec((tm, D_MODEL), cur(ca)),
            pl.BlockSpec((tm, D_MODEL), cur(cg)),
            pl.BlockSpec((HALO, D_MODEL), prev(ca)),
            pl.BlockSpec((HALO, D_MODEL), prev(cg)),
            pl.BlockSpec((HALO, D_MODEL), nxt(ca)),
            pl.BlockSpec((HALO, D_MODEL), nxt(cg)),
            full(kern.shape), full(bias.shape), full(lng.shape), full(lnb.shape),
        ],
        out_specs=pl.BlockSpec((tm, D_MODEL), lambda b, t: (b * tiles + t, 0)),
        out_shape=jax.ShapeDtypeStruct((batch * seq, D_MODEL), BF16),
        scratch_shapes=[pltpu.VMEM((tm + 2 * HALO, D_MODEL), F32), pltpu.VMEM((SUBLANES, CONV_ROWS + SUBLANES, D_MODEL), F32)],
        compiler_params=_cparams(("parallel", "parallel")),
        name="conv",
    )(p, p, p, p, p, p, kern, bias, lng, lnb)


def _head_sum_bcast(x, e_ref, et_ref):
    def two_pass(a, m_ref):
        hi = a.astype(BF16)
        lo = (a - hi.astype(F32)).astype(BF16)
        return (jnp.dot(hi, m_ref[...], preferred_element_type=F32)
                + jnp.dot(lo, m_ref[...], preferred_element_type=F32))

    return two_pass(two_pass(x, e_ref), et_ref)


def _rwkv_prep_kernel(*refs, tm, tiles, n_ctx):
    (r_c, k_c, v_c, l_c, r_p, k_p, v_p, l_p, r_n, k_n, v_n, l_n,
     mu_rkv, mu_lora, w0_ref, a0_ref, w2_ref, a2_ref, g2_ref, kk_ref, ka_ref, rk_ref, e_ref, et_ref,
     r_o, v_o, kk_o, kd_o, b_o, w_o, g_o, bonus_o) = refs
    t = pl.program_id(1)
    has_prev, has_next = _segment_edges(t, tm, tiles, n_ctx)
    row = lax.broadcasted_iota(jnp.int32, (tm, 1), 0)

    def shifted(cur, prv, nxt, mu0, mu1):
        x = cur[...].astype(F32)
        before = jnp.where(has_prev, prv[HALO - 1:HALO, :].astype(F32), 0.0)
        after = jnp.where(has_next, nxt[0:1, :].astype(F32), 0.0)
        x_prev = jnp.where(row == 0, before, pltpu.roll(x, 1, 0))
        x_next = jnp.where(row == tm - 1, after, pltpu.roll(x, tm - 1, 0))
        return x + mu0 * (x_prev - x) + mu1 * (x_next - x)

    def col(ref, i):
        return ref[:, i * D_MODEL:(i + 1) * D_MODEL]

    r = shifted(r_c, r_p, r_n, mu_rkv[0:1, 0:D_MODEL], mu_rkv[1:2, 0:D_MODEL])
    k = shifted(k_c, k_p, k_n, mu_rkv[0:1, D_MODEL:2 * D_MODEL], mu_rkv[1:2, D_MODEL:2 * D_MODEL])
    v = shifted(v_c, v_p, v_n, mu_rkv[0:1, 2 * D_MODEL:], mu_rkv[1:2, 2 * D_MODEL:])
    lo = shifted(l_c, l_p, l_n, mu_lora[0:1, :], mu_lora[1:2, :])
    w1, a1, g1 = lo[:, :LANES], lo[:, LANES:2 * LANES], lo[:, 2 * LANES:]

    lw = jnp.dot(jnp.tanh(w1).astype(BF16), w2_ref[...], preferred_element_type=F32)
    la = jnp.dot(a1.astype(BF16), a2_ref[...], preferred_element_type=F32)
    g = jnp.dot(_sigmoid(g1).astype(BF16), g2_ref[...], preferred_element_type=F32)

    kk = k * kk_ref[...]
    kk = kk * lax.rsqrt(jnp.maximum(_head_sum_bcast(kk * kk, e_ref, et_ref), 1e-24))
    kd_sum = jnp.zeros_like(k)
    for d in range(2):
        x = -(col(w0_ref, d) + col(lw, d))
        softplus = jnp.maximum(x, 0.0) + jnp.log(1.0 + jnp.exp(-jnp.abs(x)))
        w_o[d] = -jnp.exp(-softplus - 0.5)
        a = _sigmoid(col(a0_ref, d) + col(la, d))
        kd = k * (1.0 + (a - 1.0) * ka_ref[...])
        kd_sum = kd_sum + kd
        kd_o[d] = kd.astype(BF16)
        b_o[d] = (kk * a).astype(BF16)
    bonus = _head_sum_bcast(r * kd_sum * rk_ref[...], e_ref, et_ref) * v
    r_o[...] = r.astype(BF16)
    v_o[...] = v.astype(BF16)
    kk_o[...] = kk.astype(BF16)
    g_o[...] = g.astype(BF16)
    bonus_o[...] = bonus.astype(BF16)


def _rwkv_prep(p, mu_rkv, mu_lora, w0, a0, w2, a2, g2, k_k, k_a, r_k, e, et, *, batch, seq, n_ctx):
    tm = _pick_tile(n_ctx, (256, 128))
    tiles = seq // tm
    per = tm // HALO
    m = batch * seq
    last_halo = m // HALO - 1
    cur = lambda c: (lambda b, t: (b * tiles + t, c))
    prev = lambda c: (lambda b, t: (jnp.maximum((b * tiles + t) * per - 1, 0), c))
    nxt = lambda c: (lambda b, t: (jnp.minimum((b * tiles + t + 1) * per, last_halo), c))
    lora_w = 3 * LANES
    cols = [(D_MODEL, COL_R // D_MODEL), (D_MODEL, COL_K // D_MODEL), (D_MODEL, COL_V // D_MODEL),
            (lora_w, COL_LORA // lora_w)]
    in_specs = [pl.BlockSpec((tm, w), cur(c)) for w, c in cols]
    in_specs += [pl.BlockSpec((HALO, w), prev(c)) for w, c in cols]
    in_specs += [pl.BlockSpec((HALO, w), nxt(c)) for w, c in cols]
    consts = [mu_rkv, mu_lora, w0, a0, w2, a2, g2, k_k, k_a, r_k, e, et]
    in_specs += [pl.BlockSpec(c.shape, lambda b, t: (0, 0)) for c in consts]
    row_spec = pl.BlockSpec((tm, D_MODEL), lambda b, t: (b * tiles + t, 0))
    dir_spec = pl.BlockSpec((2, tm, D_MODEL), lambda b, t: (0, b * tiles + t, 0))
    tok = lambda dt: jax.ShapeDtypeStruct((m, D_MODEL), dt)
    dirs = lambda dt: jax.ShapeDtypeStruct((2, m, D_MODEL), dt)
    kfn = functools.partial(_rwkv_prep_kernel, tm=tm, tiles=tiles, n_ctx=n_ctx)
    return pl.pallas_call(
        kfn,
        grid=(batch, tiles),
        in_specs=in_specs,
        out_specs=[row_spec, row_spec, row_spec, dir_spec, dir_spec, dir_spec, row_spec, row_spec],
        out_shape=[tok(BF16), tok(BF16), tok(BF16), dirs(BF16), dirs(BF16), dirs(F32), tok(BF16), tok(BF16)],
        compiler_params=_cparams(("parallel", "parallel")),
        name="rwkv_prep",
    )(*([p] * 12), *consts)


def _nt(a, b):
    return lax.dot_general(a, b, (((1,), (1,)), ((), ())), preferred_element_type=F32)


def _tn(a, b):
    return lax.dot_general(a, b, (((0,), (0,)), ((), ())), preferred_element_type=F32)


def _mm(a, b):
    return jnp.dot(a.astype(BF16), b.astype(BF16), preferred_element_type=F32)


def _scan_kernel(rf_ref, vf_ref, kkf_ref, kdf_ref, bf_ref, wf_ref, rr_ref, vr_ref, kkr_ref, kdr_ref, br_ref, wr_ref,
                 yf_ref, yr_ref, s_ref):
    c_len = SCAN_CHUNK
    n_pairs = D_MODEL // PAIR

    @pl.when(pl.program_id(1) == 0)
    def _():
        s_ref[...] = jnp.zeros_like(s_ref)

    row = lax.broadcasted_iota(jnp.int32, (c_len, 1), 0)
    r2 = lax.broadcasted_iota(jnp.int32, (PAIR, PAIR), 0)
    c2 = lax.broadcasted_iota(jnp.int32, (PAIR, PAIR), 1)
    same = (r2 >> 6) == (c2 >> 6)
    dist = (r2 & (c_len - 1)) - (c2 & (c_len - 1))
    eye = jnp.where(r2 == c2, 1.0, 0.0)
    halves = [jnp.logical_and((r2 >> (l + 1)) == (c2 >> (l + 1)), ((r2 >> l) & 1) != ((c2 >> l) & 1))
              for l in range(6)]

    def stack(x):
        return jnp.where(same, jnp.concatenate([x, x], axis=0), 0.0)

    def dot(a, b):
        return jnp.dot(a, b, preferred_element_type=F32)

    rt, kt, vs, kh, bh, g_tot, sc, strict, incl = [], [], [], [], [], [], [], [], []
    dirs = ((rf_ref, vf_ref, kkf_ref, kdf_ref, bf_ref, wf_ref, 1),
            (rr_ref, vr_ref, kkr_ref, kdr_ref, br_ref, wr_ref, -1))
    for r_ref, v_ref, kk_ref, kd_ref, b_ref, w_ref, sign in dirs:
        w = w_ref[0]
        cum = w
        for step in (1, 2, 4, 8, 16, 32):
            if sign > 0:
                cum = cum + jnp.where(row >= step, pltpu.roll(cum, step, 0), 0.0)
            else:
                cum = cum + jnp.where(row < c_len - step, pltpu.roll(cum, c_len - step, 0), 0.0)
        tot = jnp.sum(w, axis=0, keepdims=True)
        before = jnp.logical_and(same, dist * sign > 0)
        upto = jnp.logical_and(same, dist * sign >= 0)
        for pr in range(n_pairs):
            sl = slice(pr * PAIR, (pr + 1) * PAIR)
            cum_p, w_p, tot_p = cum[:, sl], w[:, sl], tot[:, sl]
            g_inv = jnp.exp(-cum_p)
            g_tail = jnp.exp(tot_p - cum_p)
            kd = kd_ref[0, :, sl].astype(F32)
            b = b_ref[0, :, sl].astype(F32)
            rt.append(stack(r_ref[:, sl].astype(F32) * jnp.exp(cum_p)))
            kt.append(stack(kk_ref[:, sl].astype(F32) * jnp.exp(cum_p - w_p)).astype(BF16))
            vs.append(stack(v_ref[:, sl].astype(F32)).astype(BF16))
            kh.append(stack(kd * g_tail).astype(BF16))
            bh.append(stack(b * g_tail).astype(BF16))
            g_tot.append(jnp.exp(tot_p))
            strict.append(before)
            incl.append(upto)
            lhs = jnp.concatenate([kt[-1], rt[-1].astype(BF16)], axis=0)
            rhs = jnp.concatenate([stack(kd * g_inv), stack(b * g_inv)], axis=0).astype(BF16)
            sc.append(_nt(lhs, rhs))

    probs = range(2 * n_pairs)
    a_b = [jnp.where(incl[q], sc[q][PAIR:, PAIR:], 0.0).astype(BF16) for q in probs]
    mkv = [dot(jnp.where(strict[q], sc[q][:PAIR, :PAIR], 0.0).astype(BF16), vs[q]) for q in probs]
    akv = [dot(jnp.where(incl[q], sc[q][PAIR:, :PAIR], 0.0).astype(BF16), vs[q]) for q in probs]
    vtk = [_tn(vs[q], kh[q]) for q in probs]

    m_b = [jnp.where(strict[q], sc[q][:PAIR, PAIR:], 0.0) for q in probs]
    t_inv = [eye - jnp.where(halves[0], m_b[q], 0.0) for q in probs]
    for lvl in range(1, len(halves)):
        t_bf = [t_inv[q].astype(BF16) for q in probs]
        dx = [dot(t_bf[q], jnp.where(halves[lvl], m_b[q], 0.0).astype(BF16)).astype(BF16) for q in probs]
        t_inv = [t_inv[q] - dot(dx[q], t_bf[q]) for q in probs]
    t_inv = [t_inv[q].astype(BF16) for q in probs]

    wu = [dot(t_inv[q], jnp.concatenate([kt[q], mkv[q].astype(BF16)], axis=1)).astype(BF16)
          for q in probs]
    wub = [_tn(wu[q], bh[q]) for q in probs]
    abwu = [dot(a_b[q], wu[q]) for q in probs]
    s0 = [s_ref[q] for q in probs]
    s0b = [s0[q].astype(BF16) for q in probs]
    y = [_nt((rt[q] - abwu[q][:, :PAIR]).astype(BF16), s0b[q]) for q in probs]
    ds = [dot(s0b[q], wub[q][:PAIR].astype(BF16)) for q in probs]
    for q in probs:
        sl = slice((q % n_pairs) * PAIR, (q % n_pairs + 1) * PAIR)
        s_ref[q] = s0[q] * g_tot[q] + (vtk[q] - wub[q][PAIR:] - ds[q])
        y_st = y[q] + akv[q] - abwu[q][:, PAIR:]
        y_ref = yf_ref if q < n_pairs else yr_ref
        y_ref[:, sl] = y_st[:c_len] + y_st[c_len:]


def _rwkv_scan(r, v, kk, kd, b, w, *, batch, seq, n_ctx):
    c_len = SCAN_CHUNK
    n_c = seq // c_len
    n_cc = n_ctx // c_len
    m = batch * seq

    def fwd(bi, c):
        return bi * n_c + c

    def rev(bi, c):
        return bi * n_c + jnp.where(c < n_cc, n_cc - 1 - c, n_c - 1 - (c - n_cc))

    def specs(blk, d):
        tok = pl.BlockSpec((c_len, D_MODEL), lambda bi, c: (blk(bi, c), 0))
        per_dir = pl.BlockSpec((1, c_len, D_MODEL), lambda bi, c: (d, blk(bi, c), 0))
        return [tok, tok, tok, per_dir, per_dir, per_dir]

    out = jax.ShapeDtypeStruct((m, D_MODEL), F32)
    return pl.pallas_call(
        _scan_kernel,
        grid=(batch, n_c),
        in_specs=specs(fwd, 0) + specs(rev, 1),
        out_specs=[pl.BlockSpec((c_len, D_MODEL), lambda bi, c: (fwd(bi, c), 0)),
                   pl.BlockSpec((c_len, D_MODEL), lambda bi, c: (rev(bi, c), 0))],
        out_shape=[out, out],
        scratch_shapes=[pltpu.VMEM((2 * D_MODEL // PAIR, PAIR, PAIR), F32)],
        compiler_params=_cparams(("parallel", "arbitrary")),
        name="rwkv_scan",
    )(r, v, kk, kd, b, w, r, v, kk, kd, b, w)


def _mix_kernel(z_ref, mod_ref, att_ref, cv_ref, y0_ref, y1_ref, g_ref, bonus_ref, gates_ref,
                woa_ref, wpw_ref, wor_ref, wout_ref, lng_ref, lnb_ref, e_ref, et_ref, o_ref,
                *, tm, tiles, n_ctx, ctx_row):
    i = pl.program_id(0)
    y_a = jnp.dot(att_ref[...], woa_ref[...], preferred_element_type=F32)
    y_b = jnp.dot(cv_ref[...], wpw_ref[...], preferred_element_type=F32)

    y = y0_ref[...] + y1_ref[...]
    inv_n = 1.0 / RWKV_HEAD
    mu = _head_sum_bcast(y, e_ref, et_ref) * inv_n
    dlt = y - mu
    var = _head_sum_bcast(dlt * dlt, e_ref, et_ref) * inv_n
    yn = dlt * lax.rsqrt(var + GN_EPS) * lng_ref[...] + lnb_ref[...]
    rw = (yn + bonus_ref[...].astype(F32)) * g_ref[...].astype(F32)
    y_c = jnp.dot(rw.astype(BF16), wor_ref[...], preferred_element_type=F32)

    def gate(j):
        return _sigmoid(gates_ref[:, j * D_MODEL:(j + 1) * D_MODEL].astype(F32))

    mix = gate(0) * y_a + gate(1) * y_b + gate(2) * y_c
    out = jnp.dot(mix.astype(BF16), wout_ref[...], preferred_element_type=F32)
    g_msa = _mod_rows(mod_ref, 2, i // tiles, (i % tiles) * tm, tm, n_ctx, ctx_row)
    o_ref[...] = z_ref[...] + g_msa * out


def _mix_out(z, mod, att, cv, y_fwd, y_rev, g, bonus, p, woa, wpw, wor, wout, lng, lnb, e, et, *, seq, n_ctx, ctx_row):
    m, d = z.shape
    tm = _pick_tile(seq, (320, 256, 128))
    tiles = seq // tm
    row = pl.BlockSpec((tm, d), lambda i: (i, 0))
    full = lambda a: pl.BlockSpec(a.shape, lambda i: (0,) * a.ndim)
    kfn = functools.partial(_mix_kernel, tm=tm, tiles=tiles, n_ctx=n_ctx, ctx_row=ctx_row)
    return pl.pallas_call(
        kfn,
        grid=(m // tm,),
        in_specs=[
            row, full(mod), row, row,
            row, row, row, row,
            pl.BlockSpec((tm, 3 * d), lambda i: (i, COL_GATES // (3 * d))),
            full(woa), full(wpw), full(wor), full(wout), full(lng), full(lnb), full(e), full(et),
        ],
        out_specs=row,
        out_shape=jax.ShapeDtypeStruct((m, d), F32),
        compiler_params=_cparams(("parallel",)),
        name="mix_out",
    )(z, mod, att, cv, y_fwd, y_rev, g, bonus, p, woa, wpw, wor, wout, lng, lnb, e, et)


def _mlp_kernel(z_ref, mod_ref, g_ref, wu_ref, wd_ref, fg_ref, o_ref, h_ref, acc_ref,
                *, tm, tiles, n_ctx, ctx_row, final):
    i = pl.program_id(0)
    f = pl.program_id(1)
    batch, row0 = i // tiles, (i % tiles) * tm

    @pl.when(f == 0)
    def _():
        h = _adaln(z_ref[...], g_ref[...], mod_ref, 3, 4, batch, row0, tm, n_ctx, ctx_row)
        h_ref[...] = h.astype(BF16)
        acc_ref[...] = jnp.zeros_like(acc_ref)

    a = jnp.dot(h_ref[...], wu_ref[...], preferred_element_type=F32)
    a = jnp.square(jnp.maximum(a, 0.0))
    acc_ref[...] += jnp.dot(a.astype(BF16), wd_ref[...], preferred_element_type=F32)

    @pl.when(f == pl.num_programs(1) - 1)
    def _():
        out = z_ref[...] + _mod_rows(mod_ref, 5, batch, row0, tm, n_ctx, ctx_row) * acc_ref[...]
        if final:
            out = _rms(out, fg_ref[...])
        o_ref[...] = out


def _mlp(z, mod, g, wu, wd, fg, *, seq, n_ctx, ctx_row, final):
    m, d = z.shape
    ff = wu.shape[1]
    tm = _pick_tile(seq, (1280, 640, 320, 256))
    tf = 512
    kfn = functools.partial(_mlp_kernel, tm=tm, tiles=seq // tm, n_ctx=n_ctx, ctx_row=ctx_row, final=final)
    return pl.pallas_call(
        kfn,
        grid=(m // tm, ff // tf),
        in_specs=[
            pl.BlockSpec((tm, d), lambda i, f: (i, 0)),
            pl.BlockSpec(mod.shape, lambda i, f: (0, 0, 0)),
            pl.BlockSpec((1, d), lambda i, f: (0, 0)),
            pl.BlockSpec((d, tf), lambda i, f: (0, f)),
            pl.BlockSpec((tf, d), lambda i, f: (f, 0)),
            pl.BlockSpec((1, d), lambda i, f: (0, 0)),
        ],
        out_specs=pl.BlockSpec((tm, d), lambda i, f: (i, 0)),
        out_shape=jax.ShapeDtypeStruct((m, d), F32),
        scratch_shapes=[pltpu.VMEM((tm, d), BF16), pltpu.VMEM((tm, d), F32)],
        compiler_params=_cparams(("parallel", "arbitrary")),
        name="mlp",
    )(z, mod, g, wu, wd, fg)


def _layout_w_in(w_in):
    depth = w_in.shape[0]
    off_q = Q_LORA
    off_kv = off_q + KV_LORA + MLA_ROPE
    off_glu = off_kv + 2 * D_MODEL
    rw = off_glu
    rw_in = 3 * D_MODEL + 2 * DECAY_LORA + 2 * ICL_LORA + GATE_LORA
    off_rwkv = off_glu + rw_in
    seg = lambda a, b: w_in[:, :, a:b]
    k_rope = seg(Q_LORA + KV_LORA, off_kv)
    swap = jnp.arange(MLA_ROPE) ^ ROPE_FREQS
    z64 = jnp.zeros(w_in.shape[:2] + (LANES - MLA_ROPE,), w_in.dtype)
    parts = [
        seg(off_rwkv, off_rwkv + 3 * D_MODEL),
        seg(off_kv, off_kv + 2 * D_MODEL),
        seg(rw, rw + 3 * D_MODEL),
        seg(0, Q_LORA), seg(Q_LORA, Q_LORA + KV_LORA),
        k_rope, z64, k_rope[:, :, swap], z64,
        seg(rw + 3 * D_MODEL, rw + rw_in),
    ]
    w = jnp.concatenate(parts, axis=-1)
    pad = D_IN_PAD - w.shape[-1]
    return jnp.pad(w, ((0, 0), (0, 0), (0, pad))).astype(BF16)


def _layout_w_q(w_q_b):
    depth = w_q_b.shape[0]
    w = w_q_b.reshape(depth, Q_LORA, MLA_HEADS, MLA_NOPE + MLA_ROPE)
    nope, rope = w[..., :MLA_NOPE], w[..., MLA_NOPE:]
    swap = jnp.arange(MLA_ROPE) ^ ROPE_FREQS
    z64 = jnp.zeros(rope.shape[:-1] + (LANES - MLA_ROPE,), w.dtype)
    w = jnp.concatenate([nope, rope, z64, rope[..., swap], z64], axis=-1)
    return w.reshape(depth, Q_LORA, MLA_HEADS * 3 * LANES).astype(BF16)


def _rope_tables(n_ctx, n_lat):
    rows = n_lat // GRID_W
    row = jnp.repeat(jnp.arange(rows), GRID_W).astype(F32)
    colp = jnp.tile(jnp.arange(GRID_W), rows).astype(F32)
    inv_freq = ROPE_THETA ** (-jnp.arange(ROPE_FREQS, dtype=F32) / ROPE_FREQS)
    ang_lat = jnp.stack([row[:, None] * inv_freq, colp[:, None] * inv_freq], axis=1)
    ang = jnp.concatenate([jnp.zeros((n_ctx, 2, ROPE_FREQS), F32), ang_lat], axis=0)
    cos, sin = jnp.cos(ang), jnp.sin(ang)
    cos_t = jnp.stack([cos, cos], axis=2).reshape(-1, MLA_ROPE)
    sin_t = jnp.stack([-sin, sin], axis=2).reshape(-1, MLA_ROPE)
    pad = ((0, 0), (0, LANES - MLA_ROPE))
    return jnp.pad(cos_t, pad), jnp.pad(sin_t, pad)


def _block_diag2(w):
    z = jnp.zeros_like(w[0])
    return jnp.concatenate([jnp.concatenate([w[0], z], axis=1), jnp.concatenate([z, w[1]], axis=1)], axis=0)


def kernel(x, c, ctx, c_ctx, w_mod, b_mod, norm1_g, norm2_g, w_in, q_a_norm_g, w_q_b, kv_a_norm_g, w_kv_b, w_o_mla, dw_kernel, dw_bias, conv_ln_g, conv_ln_b, w_pw2, rwkv_mu, rwkv_w0, rwkv_w2, rwkv_a0, rwkv_a2, rwkv_g2, rwkv_k_k, rwkv_k_a, rwkv_r_k, rwkv_ln_g, rwkv_ln_b, w_o_rwkv, w_out, w_up, w_down, final_norm_g):
    batch, n_lat, d = x.shape
    n_ctx = ctx.shape[1]
    seq = n_ctx + n_lat
    depth = w_mod.shape[0]
    assert d == D_MODEL and batch < 8 and n_ctx % SCAN_CHUNK == 0 and seq % SCAN_CHUNK == 0
    ctx_row = batch

    xc = jnp.zeros((8, d), F32).at[:batch].set(c).at[batch].set(c_ctx)
    mod_all = _modulation(xc, w_mod, b_mod).reshape(depth, 8, 6, d).transpose(0, 2, 1, 3)

    w_in_l = _layout_w_in(w_in)
    w_q_l = _layout_w_q(w_q_b)
    w_kv_h = w_kv_b.reshape(depth, KV_LORA, MLA_HEADS, MLA_NOPE + MLA_V)
    w_k_l = w_kv_h[..., :MLA_NOPE].reshape(depth, KV_LORA, MLA_HEADS * MLA_NOPE).astype(BF16)
    w_vt_l = w_kv_h[..., MLA_NOPE:].transpose(0, 2, 3, 1).reshape(depth, MLA_HEADS * MLA_V, KV_LORA).astype(BF16)
    cos_t, sin_t = _rope_tables(n_ctx, n_lat)
    head_of = jnp.arange(D_MODEL) // RWKV_HEAD
    e = (head_of[:, None] == jnp.arange(LANES)[None, :]).astype(BF16)
    et = e.T
    rw_cols = 3 * D_MODEL
    row2 = lambda a: a.reshape(1, -1)

    z = jnp.concatenate([ctx, x], axis=1).reshape(batch * seq, d)
    for l in range(depth):
        mod = mod_all[l]
        p = _in_proj(z, mod, row2(norm1_g[l]), w_in_l[l], seq=seq, n_ctx=n_ctx, ctx_row=ctx_row)

        q, k, vt = _qkv(p, cos_t, sin_t, row2(q_a_norm_g[l]), row2(kv_a_norm_g[l]), w_q_l[l], w_k_l[l], w_vt_l[l],
                        batch=batch, seq=seq)
        att = _attention(q, k, vt, n_ctx=n_ctx).reshape(batch * seq, MLA_HEADS * MLA_V)

        cv = _conv(p, dw_kernel[l],row2(dw_bias[l]), row2(conv_ln_g[l]), row2(conv_ln_b[l]),
                   batch=batch, seq=seq, n_ctx=n_ctx)

        r_s, v_s, kk_s, kd_s, b_s, w_s, g_s, bonus = _rwkv_prep(
            p, rwkv_mu[l][:, :rw_cols], rwkv_mu[l][:, rw_cols:],
            rwkv_w0[l].reshape(1, 2 * d), rwkv_a0[l].reshape(1, 2 * d),
            _block_diag2(rwkv_w2[l]).astype(BF16), _block_diag2(rwkv_a2[l]).astype(BF16),
            rwkv_g2[l].astype(BF16), row2(rwkv_k_k[l]), row2(rwkv_k_a[l]), rwkv_r_k[l].reshape(1, d),
            e, et, batch=batch, seq=seq, n_ctx=n_ctx)
        y_fwd, y_rev = _rwkv_scan(r_s, v_s, kk_s, kd_s, b_s, w_s, batch=batch, seq=seq, n_ctx=n_ctx)

        z = _mix_out(z, mod, att, cv, y_fwd, y_rev, g_s, bonus, p,
                     w_o_mla[l].astype(BF16), w_pw2[l].astype(BF16), w_o_rwkv[l].astype(BF16),
                     w_out[l].astype(BF16), row2(rwkv_ln_g[l]), row2(rwkv_ln_b[l]), e, et,
                     seq=seq, n_ctx=n_ctx, ctx_row=ctx_row)

        z = _mlp(z, mod, row2(norm2_g[l]), w_up[l].astype(BF16), w_down[l].astype(BF16),
                 row2(final_norm_g), seq=seq, n_ctx=n_ctx, ctx_row=ctx_row, final=(l == depth - 1))

    return z.reshape(batch, seq, d)[:, n_ctx:]
```

```python
import functools
import math

import jax
import jax.numpy as jnp
from jax import lax
from jax.experimental import pallas as pl
from jax.experimental.pallas import tpu as pltpu

F32 = jnp.float32
BF16 = jnp.bfloat16
HIGHEST = lax.Precision.HIGHEST

D_MODEL = 1024
GRID_W = 64
NORM_EPS = 1e-6
LN_EPS = 1e-5
MLA_HEADS = 8
MLA_NOPE = 128
MLA_ROPE = 64
MLA_V = 128
Q_LORA = 512
KV_LORA = 256
ROPE_FREQS = MLA_ROPE // 4
ROPE_THETA = 10000.0
CONV_W = 31
RWKV_HEAD = 64
RWKV_HEADS = D_MODEL // RWKV_HEAD
DECAY_LORA = 64
ICL_LORA = 64
GATE_LORA = 128
GN_EPS = 64e-5
D_FF = 4 * D_MODEL

LANES = 128
SUBLANES = 8
BF16_SUBLANES = 16
VMEM_LIMIT = 56 * 1024 * 1024

COL_GATES = 0
COL_GLU_A = 3 * D_MODEL
COL_GLU_G = 4 * D_MODEL
COL_R = 5 * D_MODEL
COL_K = 6 * D_MODEL
COL_V = 7 * D_MODEL
COL_QA = 8 * D_MODEL
COL_KVL = COL_QA + Q_LORA
COL_KR = COL_KVL + KV_LORA
COL_LORA = COL_KR + 2 * LANES
D_IN_PAD = COL_LORA + 3 * LANES
IN_PROJ_TN = D_IN_PAD // 5
HEAD_QK = 2 * LANES

SCAN_CHUNK = 64
PAIR = 2 * RWKV_HEAD


def _cparams(sem):
    return pltpu.CompilerParams(dimension_semantics=sem, vmem_limit_bytes=VMEM_LIMIT)


def _pick_tile(n, prefs):
    for t in prefs:
        if n % t == 0:
            return t
    raise ValueError(f"no tile in {prefs} divides {n}")


def _sigmoid(x):
    return 0.5 * jnp.tanh(0.5 * x) + 0.5


def _mod_kernel(x_ref, w_ref, b_ref, o_ref):
    x = x_ref[...]
    s = x * _sigmoid(x)
    o_ref[0] = jnp.dot(s, w_ref[0], precision=HIGHEST, preferred_element_type=F32) + b_ref[0]


def _modulation(xc, w_mod, b_mod):
    depth, d, n = w_mod.shape
    tn = _pick_tile(n, (1536, 1024, 512, 128))
    return pl.pallas_call(
        _mod_kernel,
        grid=(depth, n // tn),
        in_specs=[
            pl.BlockSpec((8, d), lambda l, j: (0, 0)),
            pl.BlockSpec((1, d, tn), lambda l, j: (l, 0, j)),
            pl.BlockSpec((1, 1, tn), lambda l, j: (l, 0, j)),
        ],
        out_specs=pl.BlockSpec((1, 8, tn), lambda l, j: (l, 0, j)),
        out_shape=jax.ShapeDtypeStruct((depth, 8, n), F32),
        compiler_params=_cparams(("parallel", "parallel")),
        name="modulation",
    )(xc, w_mod, b_mod.reshape(depth, 1, n))


def _mod_rows(mod_ref, slot, batch, row0, tm, n_ctx, ctx_row):
    rows = row0 + lax.broadcasted_iota(jnp.int32, (tm, 1), 0)
    lat = mod_ref[slot, pl.ds(batch, 1), :]
    ctx = mod_ref[slot, pl.ds(ctx_row, 1), :]
    return jnp.where(rows < n_ctx, ctx, lat)


def _rms(x, g):
    return x * lax.rsqrt(jnp.mean(x * x, axis=-1, keepdims=True) + NORM_EPS) * g


def _adaln(z, g, mod_ref, s_shift, s_scale, batch, row0, tm, n_ctx, ctx_row):
    shift = _mod_rows(mod_ref, s_shift, batch, row0, tm, n_ctx, ctx_row)
    scale = _mod_rows(mod_ref, s_scale, batch, row0, tm, n_ctx, ctx_row)
    return _rms(z, g) * (1.0 + scale) + shift


def _in_proj_kernel(z_ref, mod_ref, g_ref, w_ref, o_ref, h_ref, *, tm, tiles, n_ctx, ctx_row):
    i = pl.program_id(0)

    @pl.when(pl.program_id(1) == 0)
    def _():
        h = _adaln(z_ref[...], g_ref[...], mod_ref, 0, 1, i // tiles, (i % tiles) * tm, tm, n_ctx, ctx_row)
        h_ref[...] = h.astype(BF16)

    o_ref[...] = jnp.dot(h_ref[...], w_ref[...], preferred_element_type=F32).astype(o_ref.dtype)


def _in_proj(z, mod, g, w, *, seq, n_ctx, ctx_row):
    m, d = z.shape
    n = w.shape[1]
    tm = _pick_tile(seq, (1280, 640, 320, 256))
    tn = IN_PROJ_TN
    kern = functools.partial(_in_proj_kernel, tm=tm, tiles=seq // tm, n_ctx=n_ctx, ctx_row=ctx_row)
    return pl.pallas_call(
        kern,
        grid=(m // tm, n // tn),
        in_specs=[
            pl.BlockSpec((tm, d), lambda i, j: (i, 0)),
            pl.BlockSpec(mod.shape, lambda i, j: (0, 0, 0)),
            pl.BlockSpec((1, d), lambda i, j: (0, 0)),
            pl.BlockSpec((d, tn), lambda i, j: (0, j)),
        ],
        out_specs=pl.BlockSpec((tm, tn), lambda i, j: (i, j)),
        out_shape=jax.ShapeDtypeStruct((m, n), BF16),
        scratch_shapes=[pltpu.VMEM((tm, d), BF16)],
        compiler_params=_cparams(("parallel", "arbitrary")),
        name="in_proj",
    )(z, mod, g, w)


VT_ROWS = MLA_V + BF16_SUBLANES


def _qkv_kernel(qa_ref, kvl_ref, kr_ref, cos_ref, sin_ref, qg_ref, kvg_ref, wq_ref, wk_ref, wvt_ref,
                q_ref, k_ref, vt_ref, *, q_scale):
    cos = cos_ref[...]
    sin = sin_ref[...]
    qn = _rms(qa_ref[...].astype(F32), qg_ref[...]).astype(BF16)
    kvn = _rms(kvl_ref[...].astype(F32), kvg_ref[...]).astype(BF16)
    kr = kr_ref[...].astype(F32)
    k_rope = (kr[:, :LANES] * cos + kr[:, LANES:] * sin).astype(BF16)
    tm = qn.shape[0]
    ones_rows = jnp.where(lax.broadcasted_iota(jnp.int32, (BF16_SUBLANES, tm), 0) == 0, 1.0, 0.0).astype(BF16)
    for h in range(MLA_HEADS):
        qh = jnp.dot(qn, wq_ref[:, h * 3 * LANES:(h + 1) * 3 * LANES], preferred_element_type=F32)
        rope = qh[:, LANES:2 * LANES] * cos + qh[:, 2 * LANES:] * sin
        q_ref[0, h, :, :LANES] = (qh[:, :LANES] * q_scale).astype(BF16)
        q_ref[0, h, :, LANES:] = (rope * q_scale).astype(BF16)
        kh = jnp.dot(kvn, wk_ref[:, h * MLA_NOPE:(h + 1) * MLA_NOPE], preferred_element_type=F32)
        k_ref[0, h, :, :LANES] = kh.astype(BF16)
        k_ref[0, h, :, LANES:] = k_rope
        vt = _nt(wvt_ref[h * MLA_V:(h + 1) * MLA_V, :], kvn)
        vt_ref[0, h, :MLA_V, :] = vt.astype(BF16)
        vt_ref[0, h, MLA_V:, :] = ones_rows


def _qkv(p, cos_t, sin_t, qg, kvg, wq, wk, wvt, *, batch, seq):
    tm = _pick_tile(seq, (640, 256))
    tiles = seq // tm
    q_scale = (MLA_NOPE + MLA_ROPE) ** -0.5 * math.log2(math.e)
    kern = functools.partial(_qkv_kernel, q_scale=q_scale)
    row = lambda b, t: b * tiles + t
    return pl.pallas_call(
        kern,
        grid=(batch, tiles),
        in_specs=[
            pl.BlockSpec((tm, Q_LORA), lambda b, t: (row(b, t), COL_QA // Q_LORA)),
            pl.BlockSpec((tm, KV_LORA), lambda b, t: (row(b, t), COL_KVL // KV_LORA)),
            pl.BlockSpec((tm, 2 * LANES), lambda b, t: (row(b, t), COL_KR // (2 * LANES))),
            pl.BlockSpec((tm, LANES), lambda b, t: (t, 0)),
            pl.BlockSpec((tm, LANES), lambda b, t: (t, 0)),
            pl.BlockSpec((1, Q_LORA), lambda b, t: (0, 0)),
            pl.BlockSpec((1, KV_LORA), lambda b, t: (0, 0)),
            pl.BlockSpec(wq.shape, lambda b, t: (0, 0)),
            pl.BlockSpec(wk.shape, lambda b, t: (0, 0)),
            pl.BlockSpec(wvt.shape, lambda b, t: (0, 0)),
        ],
        out_specs=[
            pl.BlockSpec((1, MLA_HEADS, tm, HEAD_QK), lambda b, t: (b, 0, t, 0)),
            pl.BlockSpec((1, MLA_HEADS, tm, HEAD_QK), lambda b, t: (b, 0, t, 0)),
            pl.BlockSpec((1, MLA_HEADS, VT_ROWS, tm), lambda b, t: (b, 0, 0, t)),
        ],
        out_shape=[
            jax.ShapeDtypeStruct((batch, MLA_HEADS, seq, HEAD_QK), BF16),
            jax.ShapeDtypeStruct((batch, MLA_HEADS, seq, HEAD_QK), BF16),
            jax.ShapeDtypeStruct((batch, MLA_HEADS, VT_ROWS, seq), BF16),
        ],
        compiler_params=_cparams(("parallel", "parallel")),
        name="qkv",
    )(p, p, p, cos_t, sin_t, qg, kvg, wq, wk, wvt)


ATT_TQ = 256
NEG_BIG = -(2.0 ** 100)


def _scores(kb, q):
    s16 = _nt(kb, q).astype(BF16)
    return s16, jnp.max(s16, axis=0, keepdims=True).astype(F32)


def _probs(s16, s_max, m):
    m_new = jnp.maximum(m, s_max)
    return m_new, jnp.exp2(s16 - m_new.astype(BF16))


def _attn_out(acc):
    return (acc[:MLA_V] / acc[MLA_V:MLA_V + 1]).T


def _attn_kernel(q_ref, k_ref, vt_ref, o_ref, s_ref, smax_ref, m_ref, acc_ref, *, chains, tk, seq):
    tq = ATT_TQ
    nblk = seq // tk

    def q_chain(j):
        return q_ref[0, 0, j * tq:(j + 1) * tq, :]

    def k_block(c):
        return k_ref[0, 0, pl.ds(pl.multiple_of(c * tk, tk), tk), :]

    def step(c, cur, prefetch):
        vtb = vt_ref[0, 0, :, pl.ds(pl.multiple_of(c * tk, tk), tk)]
        kb_next = k_block(c + 1) if prefetch else None
        for j in range(chains):
            m = m_ref[j]
            m_new, p = _probs(s_ref[cur, j], smax_ref[cur, j], m)
            if prefetch:
                s_ref[1 - cur, j], smax_ref[1 - cur, j] = _scores(kb_next, q_chain(j))
            acc_ref[j] = jnp.exp2(m - m_new) * acc_ref[j] + jnp.dot(vtb, p, preferred_element_type=F32)
            m_ref[j] = m_new

    kb0 = k_block(0)
    for j in range(chains):
        s_ref[0, j], smax_ref[0, j] = _scores(kb0, q_chain(j))
        m_ref[j] = jnp.full((1, tq), NEG_BIG, F32)
        acc_ref[j] = jnp.zeros((VT_ROWS, tq), F32)

    def body(i, carry):
        step(2 * i, 0, True)
        step(2 * i + 1, 1, True)
        return carry

    lax.fori_loop(0, (nblk - 1) // 2, body, 0)
    if (nblk - 1) % 2:
        step(nblk - 2, 0, True)
    step(nblk - 1, (nblk - 1) % 2, False)
    for j in range(chains):
        o_ref[0, j * tq:(j + 1) * tq, :] = _attn_out(acc_ref[j]).astype(o_ref.dtype)


def _attn_ctx_kernel(q_ref, k_ref, vt_ref, prev_ref, o_ref):
    del prev_ref
    s16, s_max = _scores(k_ref[0, 0], q_ref[0, 0])
    _, p = _probs(s16, s_max, s_max)
    o_ref[0] = _attn_out(jnp.dot(vt_ref[0, 0], p, preferred_element_type=F32)).astype(o_ref.dtype)


def _attention(q, k, vt, *, n_ctx):
    batch, heads, seq, _ = q.shape
    tile = _pick_tile(seq, (1280, 256))
    tk = _pick_tile(seq, (1280, 256))
    chains = tile // ATT_TQ
    kern = functools.partial(_attn_kernel, chains=chains, tk=tk, seq=seq)
    out_shape = jax.ShapeDtypeStruct((batch, seq, heads * MLA_V), BF16)
    att = pl.pallas_call(
        kern,
        grid=(batch, heads, seq // tile),
        in_specs=[
            pl.BlockSpec((1, 1, tile, HEAD_QK), lambda b, h, i: (b, h, i, 0)),
            pl.BlockSpec((1, 1, seq, HEAD_QK), lambda b, h, i: (b, h, 0, 0)),
            pl.BlockSpec((1, 1, VT_ROWS, seq), lambda b, h, i: (b, h, 0, 0)),
        ],
        out_specs=pl.BlockSpec((1, tile, MLA_V), lambda b, h, i: (b, i, h)),
        out_shape=out_shape,
        scratch_shapes=[pltpu.VMEM((2, chains, tk, ATT_TQ), BF16), pltpu.VMEM((2, chains, 1, ATT_TQ), F32),
                        pltpu.VMEM((chains, 1, ATT_TQ), F32), pltpu.VMEM((chains, VT_ROWS, ATT_TQ), F32)],
        compiler_params=_cparams(("parallel", "parallel", "arbitrary")),
        name="attention",
    )(q, k, vt)
    return pl.pallas_call(
        _attn_ctx_kernel,
        grid=(batch, heads),
        in_specs=[
            pl.BlockSpec((1, 1, n_ctx, HEAD_QK), lambda b, h: (b, h, 0, 0)),
            pl.BlockSpec((1, 1, n_ctx, HEAD_QK), lambda b, h: (b, h, 0, 0)),
            pl.BlockSpec((1, 1, VT_ROWS, n_ctx), lambda b, h: (b, h, 0, 0)),
            pl.BlockSpec(memory_space=pl.ANY),
        ],
        out_specs=pl.BlockSpec((1, n_ctx, MLA_V), lambda b, h: (b, 0, h)),
        out_shape=out_shape,
        input_output_aliases={3: 0},
        compiler_params=_cparams(("parallel", "parallel")),
        name="attention_ctx",
    )(q, k, vt, att)


HALO = BF16_SUBLANES
CONV_ROWS = 64


def _segment_edges(t, tm, tiles, n_ctx):
    start = t * tm
    has_prev = jnp.logical_and(start != 0, start != n_ctx)
    has_next = jnp.logical_and(start + tm != n_ctx, t != tiles - 1)
    return has_prev, has_next


def _conv_kernel(a_ref, g_ref, ap_ref, gp_ref, an_ref, gn_ref, kern_ref, bias_ref, lng_ref, lnb_ref,
                 o_ref, buf_ref, part_ref, *, tm, tiles, n_ctx):
    t = pl.program_id(1)
    has_prev, has_next = _segment_edges(t, tm, tiles, n_ctx)

    def glu(a, g):
        return a[...].astype(F32) * _sigmoid(g[...].astype(F32))

    buf_ref[0:HALO, :] = jnp.where(has_prev, glu(ap_ref, gp_ref), 0.0)
    buf_ref[HALO:HALO + tm, :] = glu(a_ref, g_ref)
    buf_ref[HALO + tm:, :] = jnp.where(has_next, glu(an_ref, gn_ref), 0.0)

    assert HALO - CONV_W // 2 == 1
    rb = CONV_ROWS
    sub = SUBLANES
    n_a = (CONV_W + sub) // sub
    for r0 in range(0, tm, rb):
        for lg in range(D_MODEL // LANES):
            ls = slice(lg * LANES, (lg + 1) * LANES)
            xs = [buf_ref[r0 + sub * a:r0 + sub * a + rb + sub, ls] for a in range(n_a)]
            for b in range(sub):
                part = None
                for a in range(n_a):
                    u = sub * a + b
                    if 1 <= u <= CONV_W:
                        term = xs[a] * pltpu.repeat(kern_ref[u - 1, :, ls], (rb + sub) // sub, 0)
                        part = term if part is None else part + term
                part_ref[b, :, ls] = part
        acc = bias_ref[...] + part_ref[0, 0:rb, :]
        for b in range(1, sub):
            acc = acc + part_ref[b, b:b + rb, :]
        mu = jnp.mean(acc, axis=-1, keepdims=True)
        dlt = acc - mu
        var = jnp.mean(dlt * dlt, axis=-1, keepdims=True)
        y = dlt * lax.rsqrt(var + LN_EPS) * lng_ref[...] + lnb_ref[...]
        o_ref[r0:r0 + rb, :] = (y * _sigmoid(y)).astype(o_ref.dtype)


def _conv(p, kern, bias, lng, lnb, *, batch, seq, n_ctx):
    tm = _pick_tile(n_ctx, (256, 128))
    tiles = seq // tm
    per = tm // HALO
    last_halo = batch * seq // HALO - 1
    ca, cg = COL_GLU_A // D_MODEL, COL_GLU_G // D_MODEL
    cur = lambda c: (lambda b, t: (b * tiles + t, c))
    prev = lambda c: (lambda b, t: (jnp.maximum((b * tiles + t) * per - 1, 0), c))
    nxt = lambda c: (lambda b, t: (jnp.minimum((b * tiles + t + 1) * per, last_halo), c))
    full = lambda shape: pl.BlockSpec(shape, lambda b, t: (0,) * len(shape))
    kern = jnp.broadcast_to(kern[:, None, :], (CONV_W, SUBLANES, D_MODEL))
    kfn = functools.partial(_conv_kernel, tm=tm, tiles=tiles, n_ctx=n_ctx)
    return pl.pallas_call(
        kfn,
        grid=(batch, tiles),
        in_specs=[
            pl.BlockSpec((tm, D_MODEL), cur(ca)),
            pl.BlockSpec((tm, D_MODEL), cur(cg)),
            pl.BlockSpec((HALO, D_MODEL), prev(ca)),
            pl.BlockSpec((HALO, D_MODEL), prev(cg)),
            pl.BlockSpec((HALO, D_MODEL), nxt(ca)),
            pl.BlockSpec((HALO, D_MODEL), nxt(cg)),
            full(kern.shape), full(bias.shape), full(lng.shape), full(lnb.shape),
        ],
        out_specs=pl.BlockSpec((tm, D_MODEL), lambda b, t: (b * tiles + t, 0)),
        out_shape=jax.ShapeDtypeStruct((batch * seq, D_MODEL), BF16),
        scratch_shapes=[pltpu.VMEM((tm + 2 * HALO, D_MODEL), F32), pltpu.VMEM((SUBLANES, CONV_ROWS + SUBLANES, D_MODEL), F32)],
        compiler_params=_cparams(("parallel", "parallel")),
        name="conv",
    )(p, p, p, p, p, p, kern, bias, lng, lnb)


def _head_sums_bcast(xs, e_ref, et_ref):
    def two_pass(a, m_ref):
        hi = a.astype(BF16)
        lo = (a - hi.astype(F32)).astype(BF16)
        return (jnp.dot(hi, m_ref[...], preferred_element_type=F32)
                + jnp.dot(lo, m_ref[...], preferred_element_type=F32))

    sums = [two_pass(x, e_ref) for x in xs]
    return [two_pass(v, et_ref) for v in sums]


def _head_sum_bcast(x, e_ref, et_ref):
    return _head_sums_bcast([x], e_ref, et_ref)[0]


def _rwkv_prep_kernel(*refs, tm, tiles, n_ctx):
    (r_c, k_c, v_c, l_c, r_p, k_p, v_p, l_p, r_n, k_n, v_n, l_n,
     mu_rkv, mu_lora, w0_ref, a0_ref, w2_ref, a2_ref, g2_ref, kk_ref, ka_ref, rk_ref, e_ref, et_ref,
     r_o, v_o, kk_o, kd_o, b_o, w_o, g_o, bonus_o) = refs
    t = pl.program_id(1)
    has_prev, has_next = _segment_edges(t, tm, tiles, n_ctx)
    row = lax.broadcasted_iota(jnp.int32, (tm, 1), 0)

    def shifted(cur, prv, nxt, mu0, mu1):
        x = cur[...].astype(F32)
        before = jnp.where(has_prev, prv[HALO - 1:HALO, :].astype(F32), 0.0)
        after = jnp.where(has_next, nxt[0:1, :].astype(F32), 0.0)
        x_prev = jnp.where(row == 0, before, pltpu.roll(x, 1, 0))
        x_next = jnp.where(row == tm - 1, after, pltpu.roll(x, tm - 1, 0))
        return x + mu0 * (x_prev - x) + mu1 * (x_next - x)

    def col(ref, i):
        return ref[:, i * D_MODEL:(i + 1) * D_MODEL]

    r = shifted(r_c, r_p, r_n, mu_rkv[0:1, 0:D_MODEL], mu_rkv[1:2, 0:D_MODEL])
    k = shifted(k_c, k_p, k_n, mu_rkv[0:1, D_MODEL:2 * D_MODEL], mu_rkv[1:2, D_MODEL:2 * D_MODEL])
    v = shifted(v_c, v_p, v_n, mu_rkv[0:1, 2 * D_MODEL:], mu_rkv[1:2, 2 * D_MODEL:])
    lo = shifted(l_c, l_p, l_n, mu_lora[0:1, :], mu_lora[1:2, :])
    w1, a1, g1 = lo[:, :LANES], lo[:, LANES:2 * LANES], lo[:, 2 * LANES:]

    lw = jnp.dot(jnp.tanh(w1).astype(BF16), w2_ref[...], preferred_element_type=F32)
    la = jnp.dot(a1.astype(BF16), a2_ref[...], preferred_element_type=F32)
    g = jnp.dot(_sigmoid(g1).astype(BF16), g2_ref[...], preferred_element_type=F32)

    kk = k * kk_ref[...]
    kk = kk * lax.rsqrt(jnp.maximum(_head_sum_bcast(kk * kk, e_ref, et_ref), 1e-24))
    kd_sum = jnp.zeros_like(k)
    for d in range(2):
        x = -(col(w0_ref, d) + col(lw, d))
        softplus = jnp.maximum(x, 0.0) + jnp.log(1.0 + jnp.exp(-jnp.abs(x)))
        w_o[d] = -jnp.exp(-softplus - 0.5)
        a = _sigmoid(col(a0_ref, d) + col(la, d))
        kd = k * (1.0 + (a - 1.0) * ka_ref[...])
        kd_sum = kd_sum + kd
        kd_o[d] = kd.astype(BF16)
        b_o[d] = (kk * a).astype(BF16)
    bonus = _head_sum_bcast(r * kd_sum * rk_ref[...], e_ref, et_ref) * v
    r_o[...] = r.astype(BF16)
    v_o[...] = v.astype(BF16)
    kk_o[...] = kk.astype(BF16)
    g_o[...] = g.astype(BF16)
    bonus_o[...] = bonus.astype(BF16)


def _rwkv_prep(p, mu_rkv, mu_lora, w0, a0, w2, a2, g2, k_k, k_a, r_k, e, et, *, batch, seq, n_ctx):
    tm = _pick_tile(n_ctx, (256, 128))
    tiles = seq // tm
    per = tm // HALO
    m = batch * seq
    last_halo = m // HALO - 1
    cur = lambda c: (lambda b, t: (b * tiles + t, c))
    prev = lambda c: (lambda b, t: (jnp.maximum((b * tiles + t) * per - 1, 0), c))
    nxt = lambda c: (lambda b, t: (jnp.minimum((b * tiles + t + 1) * per, last_halo), c))
    lora_w = 3 * LANES
    cols = [(D_MODEL, COL_R // D_MODEL), (D_MODEL, COL_K // D_MODEL), (D_MODEL, COL_V // D_MODEL),
            (lora_w, COL_LORA // lora_w)]
    in_specs = [pl.BlockSpec((tm, w), cur(c)) for w, c in cols]
    in_specs += [pl.BlockSpec((HALO, w), prev(c)) for w, c in cols]
    in_specs += [pl.BlockSpec((HALO, w), nxt(c)) for w, c in cols]
    consts = [mu_rkv, mu_lora, w0, a0, w2, a2, g2, k_k, k_a, r_k, e, et]
    in_specs += [pl.BlockSpec(c.shape, lambda b, t: (0, 0)) for c in consts]
    row_spec = pl.BlockSpec((tm, D_MODEL), lambda b, t: (b * tiles + t, 0))
    dir_spec = pl.BlockSpec((2, tm, D_MODEL), lambda b, t: (0, b * tiles + t, 0))
    tok = lambda dt: jax.ShapeDtypeStruct((m, D_MODEL), dt)
    dirs = lambda dt: jax.ShapeDtypeStruct((2, m, D_MODEL), dt)
    kfn = functools.partial(_rwkv_prep_kernel, tm=tm, tiles=tiles, n_ctx=n_ctx)
    return pl.pallas_call(
        kfn,
        grid=(batch, tiles),
        in_specs=in_specs,
        out_specs=[row_spec, row_spec, row_spec, dir_spec, dir_spec, dir_spec, row_spec, row_spec],
        out_shape=[tok(BF16), tok(BF16), tok(BF16), dirs(BF16), dirs(BF16), dirs(F32), tok(BF16), tok(BF16)],
        compiler_params=_cparams(("parallel", "parallel")),
        name="rwkv_prep",
    )(*([p] * 12), *consts)


def _nt(a, b):
    return lax.dot_general(a, b, (((1,), (1,)), ((), ())), preferred_element_type=F32)


def _tn(a, b):
    return lax.dot_general(a, b, (((0,), (0,)), ((), ())), preferred_element_type=F32)


def _mm(a, b):
    return jnp.dot(a.astype(BF16), b.astype(BF16), preferred_element_type=F32)


def _scan_kernel(rf_ref, vf_ref, kkf_ref, kdf_ref, bf_ref, wf_ref, rr_ref, vr_ref, kkr_ref, kdr_ref, br_ref, wr_ref,
                 yf_ref, yr_ref, s_ref):
    c_len = SCAN_CHUNK
    n_pairs = D_MODEL // PAIR
    batch = rf_ref.shape[0]

    @pl.when(pl.program_id(0) == 0)
    def _():
        s_ref[...] = jnp.zeros_like(s_ref)

    row = lax.broadcasted_iota(jnp.int32, (c_len, 1), 0)
    r2 = lax.broadcasted_iota(jnp.int32, (PAIR, PAIR), 0)
    c2 = lax.broadcasted_iota(jnp.int32, (PAIR, PAIR), 1)
    same = (r2 >> 6) == (c2 >> 6)
    dist = (r2 & (c_len - 1)) - (c2 & (c_len - 1))
    eye = jnp.where(r2 == c2, 1.0, 0.0)
    halves = [jnp.logical_and((r2 >> (l + 1)) == (c2 >> (l + 1)), ((r2 >> l) & 1) != ((c2 >> l) & 1))
              for l in range(6)]

    def stack(x):
        return jnp.where(same, jnp.concatenate([x, x], axis=0), 0.0)

    def dot(a, b):
        return jnp.dot(a, b, preferred_element_type=F32)

    rt, kt, vs, kh, bh, g_tot, sc, strict, incl = [], [], [], [], [], [], [], [], []
    dirs = ((rf_ref, vf_ref, kkf_ref, kdf_ref, bf_ref, wf_ref, 1),
            (rr_ref, vr_ref, kkr_ref, kdr_ref, br_ref, wr_ref, -1))
    for (r_ref, v_ref, kk_ref, kd_ref, b_ref, w_ref, sign), bi in [(d, bi) for d in dirs for bi in range(batch)]:
        w = w_ref[0, bi]
        cum = w
        for step in (1, 2, 4, 8, 16, 32):
            if sign > 0:
                cum = cum + jnp.where(row >= step, pltpu.roll(cum, step, 0), 0.0)
            else:
                cum = cum + jnp.where(row < c_len - step, pltpu.roll(cum, c_len - step, 0), 0.0)
        tot = jnp.sum(w, axis=0, keepdims=True)
        before = jnp.logical_and(same, dist * sign > 0)
        upto = jnp.logical_and(same, dist * sign >= 0)
        for pr in range(n_pairs):
            sl = slice(pr * PAIR, (pr + 1) * PAIR)
            cum_p, w_p, tot_p = cum[:, sl], w[:, sl], tot[:, sl]
            g_inv = jnp.exp(-cum_p)
            g_tail = jnp.exp(tot_p - cum_p)
            kd = kd_ref[0, bi, :, sl].astype(F32)
            b = b_ref[0, bi, :, sl].astype(F32)
            rt.append(stack(r_ref[bi, :, sl].astype(F32) * jnp.exp(cum_p)))
            kt.append(stack(kk_ref[bi, :, sl].astype(F32) * jnp.exp(cum_p - w_p)).astype(BF16))
            vs.append(stack(v_ref[bi, :, sl].astype(F32)).astype(BF16))
            kh.append(stack(kd * g_tail).astype(BF16))
            bh.append(stack(b * g_tail).astype(BF16))
            g_tot.append(jnp.exp(tot_p))
            strict.append(before)
            incl.append(upto)
            lhs = jnp.concatenate([kt[-1], rt[-1].astype(BF16)], axis=0)
            rhs = jnp.concatenate([stack(kd * g_inv), stack(b * g_inv)], axis=0).astype(BF16)
            sc.append(_nt(lhs, rhs))

    probs = range(2 * batch * n_pairs)
    a_b = [jnp.where(incl[q], sc[q][PAIR:, PAIR:], 0.0).astype(BF16) for q in probs]
    mkv = [dot(jnp.where(strict[q], sc[q][:PAIR, :PAIR], 0.0).astype(BF16), vs[q]) for q in probs]
    akv = [dot(jnp.where(incl[q], sc[q][PAIR:, :PAIR], 0.0).astype(BF16), vs[q]) for q in probs]
    vtk = [_tn(vs[q], kh[q]) for q in probs]

    m_b = [jnp.where(strict[q], sc[q][:PAIR, PAIR:], 0.0) for q in probs]
    t_inv = [eye - jnp.where(halves[0], m_b[q], 0.0) for q in probs]
    for lvl in range(1, len(halves)):
        t_bf = [t_inv[q].astype(BF16) for q in probs]
        dx = [dot(t_bf[q], jnp.where(halves[lvl], m_b[q], 0.0).astype(BF16)).astype(BF16) for q in probs]
        t_inv = [t_inv[q] - dot(dx[q], t_bf[q]) for q in probs]
    t_inv = [t_inv[q].astype(BF16) for q in probs]

    wu = [dot(t_inv[q], jnp.concatenate([kt[q], mkv[q].astype(BF16)], axis=1)).astype(BF16)
          for q in probs]
    wub = [_tn(wu[q], bh[q]) for q in probs]
    abwu = [dot(a_b[q], wu[q]) for q in probs]
    s0 = [s_ref[q] for q in probs]
    s0b = [s0[q].astype(BF16) for q in probs]
    y = [_nt((rt[q] - abwu[q][:, :PAIR]).astype(BF16), s0b[q]) for q in probs]
    ds = [dot(s0b[q], wub[q][:PAIR].astype(BF16)) for q in probs]
    for q in probs:
        sl = slice((q % n_pairs) * PAIR, (q % n_pairs + 1) * PAIR)
        s_ref[q] = s0[q] * g_tot[q] + (vtk[q] - wub[q][PAIR:] - ds[q])
        y_st = y[q] + akv[q] - abwu[q][:, PAIR:]
        y_ref = yf_ref if q < batch * n_pairs else yr_ref
        y_ref[(q // n_pairs) % batch, :, sl] = y_st[:c_len] + y_st[c_len:]


def _rwkv_scan(r, v, kk, kd, b, w, *, batch, seq, n_ctx):
    c_len = SCAN_CHUNK
    n_c = seq // c_len
    n_cc = n_ctx // c_len

    def fwd(c):
        return c

    def rev(c):
        return jnp.where(c < n_cc, n_cc - 1 - c, n_c - 1 - (c - n_cc))

    def specs(blk, d):
        tok = pl.BlockSpec((batch, c_len, D_MODEL), lambda c: (0, blk(c), 0))
        per_dir = pl.BlockSpec((1, batch, c_len, D_MODEL), lambda c: (d, 0, blk(c), 0))
        return [tok, tok, tok, per_dir, per_dir, per_dir]

    tok3 = lambda a: a.reshape(batch, seq, D_MODEL)
    dir4 = lambda a: a.reshape(2, batch, seq, D_MODEL)
    args = (tok3(r), tok3(v), tok3(kk), dir4(kd), dir4(b), dir4(w))
    out = jax.ShapeDtypeStruct((batch, seq, D_MODEL), F32)
    y_fwd, y_rev = pl.pallas_call(
        _scan_kernel,
        grid=(n_c,),
        in_specs=specs(fwd, 0) + specs(rev, 1),
        out_specs=[pl.BlockSpec((batch, c_len, D_MODEL), lambda c: (0, fwd(c), 0)),
                   pl.BlockSpec((batch, c_len, D_MODEL), lambda c: (0, rev(c), 0))],
        out_shape=[out, out],
        scratch_shapes=[pltpu.VMEM((2 * batch * D_MODEL // PAIR, PAIR, PAIR), F32)],
        compiler_params=_cparams(("arbitrary",)),
        name="rwkv_scan",
    )(*args, *args)
    return y_fwd.reshape(batch * seq, D_MODEL), y_rev.reshape(batch * seq, D_MODEL)


def _mix_kernel(z_ref, mod_ref, att_ref, cv_ref, y0_ref, y1_ref, g_ref, bonus_ref, gates_ref,
                woa_ref, wpw_ref, wor_ref, wout_ref, lng_ref, lnb_ref, e_ref, et_ref, o_ref,
                *, tm, tiles, n_ctx, ctx_row):
    i = pl.program_id(0)
    half = tm // 2
    rows = [slice(0, half), slice(half, tm)]
    inv_n = 1.0 / RWKV_HEAD

    def dot(a, w_ref):
        return jnp.dot(a, w_ref[...], preferred_element_type=F32)

    y_a = [dot(att_ref[r, :], woa_ref) for r in rows]
    y_b = [dot(cv_ref[r, :], wpw_ref) for r in rows]
    y = [y0_ref[r, :] + y1_ref[r, :] for r in rows]
    mu = _head_sums_bcast(y, e_ref, et_ref)
    dlt = [y[k] - mu[k] * inv_n for k in range(2)]
    var = _head_sums_bcast([d * d for d in dlt], e_ref, et_ref)
    y_c = []
    for k, r in enumerate(rows):
        yn = dlt[k] * lax.rsqrt(var[k] * inv_n + GN_EPS) * lng_ref[...] + lnb_ref[...]
        rw = (yn + bonus_ref[r, :].astype(F32)) * g_ref[r, :].astype(F32)
        y_c.append(dot(rw.astype(BF16), wor_ref))
    for k, r in enumerate(rows):
        def gate(j):
            return _sigmoid(gates_ref[r, j * D_MODEL:(j + 1) * D_MODEL].astype(F32))

        mix = gate(0) * y_a[k] + gate(1) * y_b[k] + gate(2) * y_c[k]
        out = dot(mix.astype(BF16), wout_ref)
        g_msa = _mod_rows(mod_ref, 2, i // tiles, (i % tiles) * tm + k * half, half, n_ctx, ctx_row)
        o_ref[r, :] = z_ref[r, :] + g_msa * out


def _mix_out(z, mod, att, cv, y_fwd, y_rev, g, bonus, p, woa, wpw, wor, wout, lng, lnb, e, et, *, seq, n_ctx, ctx_row):
    m, d = z.shape
    tm = _pick_tile(seq, (320, 256, 128))
    tiles = seq // tm
    row = pl.BlockSpec((tm, d), lambda i: (i, 0))
    full = lambda a: pl.BlockSpec(a.shape, lambda i: (0,) * a.ndim)
    kfn = functools.partial(_mix_kernel, tm=tm, tiles=tiles, n_ctx=n_ctx, ctx_row=ctx_row)
    return pl.pallas_call(
        kfn,
        grid=(m // tm,),
        in_specs=[
            row, full(mod), row, row,
            row, row, row, row,
            pl.BlockSpec((tm, 3 * d), lambda i: (i, COL_GATES // (3 * d))),
            full(woa), full(wpw), full(wor), full(wout), full(lng), full(lnb), full(e), full(et),
        ],
        out_specs=row,
        out_shape=jax.ShapeDtypeStruct((m, d), F32),
        compiler_params=_cparams(("parallel",)),
        name="mix_out",
    )(z, mod, att, cv, y_fwd, y_rev, g, bonus, p, woa, wpw, wor, wout, lng, lnb, e, et)


def _mlp_kernel(z_ref, mod_ref, g_ref, wu_ref, wd_ref, fg_ref, o_ref, h_ref, acc_ref,
                *, tm, tiles, n_ctx, ctx_row, final):
    i = pl.program_id(0)
    f = pl.program_id(1)
    batch, row0 = i // tiles, (i % tiles) * tm

    @pl.when(f == 0)
    def _():
        h = _adaln(z_ref[...], g_ref[...], mod_ref, 3, 4, batch, row0, tm, n_ctx, ctx_row)
        h_ref[...] = h.astype(BF16)
        acc_ref[...] = jnp.zeros_like(acc_ref)

    a = jnp.dot(h_ref[...], wu_ref[...], preferred_element_type=F32)
    a = jnp.square(jnp.maximum(a, 0.0))
    acc_ref[...] += jnp.dot(a.astype(BF16), wd_ref[...], preferred_element_type=F32)

    @pl.when(f == pl.num_programs(1) - 1)
    def _():
        out = z_ref[...] + _mod_rows(mod_ref, 5, batch, row0, tm, n_ctx, ctx_row) * acc_ref[...]
        if final:
            out = _rms(out, fg_ref[...])
        o_ref[...] = out


def _mlp(z, mod, g, wu, wd, fg, *, seq, n_ctx, ctx_row, final):
    m, d = z.shape
    ff = wu.shape[1]
    tm = _pick_tile(seq, (1280, 640, 320, 256))
    tf = 512
    kfn = functools.partial(_mlp_kernel, tm=tm, tiles=seq // tm, n_ctx=n_ctx, ctx_row=ctx_row, final=final)
    return pl.pallas_call(
        kfn,
        grid=(m // tm, ff // tf),
        in_specs=[
            pl.BlockSpec((tm, d), lambda i, f: (i, 0)),
            pl.BlockSpec(mod.shape, lambda i, f: (0, 0, 0)),
            pl.BlockSpec((1, d), lambda i, f: (0, 0)),
            pl.BlockSpec((d, tf), lambda i, f: (0, f)),
            pl.BlockSpec((tf, d), lambda i, f: (f, 0)),
            pl.BlockSpec((1, d), lambda i, f: (0, 0)),
        ],
        out_specs=pl.BlockSpec((tm, d), lambda i, f: (i, 0)),
        out_shape=jax.ShapeDtypeStruct((m, d), F32),
        scratch_shapes=[pltpu.VMEM((tm, d), BF16), pltpu.VMEM((tm, d), F32)],
        compiler_params=_cparams(("parallel", "arbitrary")),
        name="mlp",
    )(z, mod, g, wu, wd, fg)


def _layout_w_in(w_in):
    depth = w_in.shape[0]
    off_q = Q_LORA
    off_kv = off_q + KV_LORA + MLA_ROPE
    off_glu = off_kv + 2 * D_MODEL
    rw = off_glu
    rw_in = 3 * D_MODEL + 2 * DECAY_LORA + 2 * ICL_LORA + GATE_LORA
    off_rwkv = off_glu + rw_in
    seg = lambda a, b: w_in[:, :, a:b]
    k_rope = seg(Q_LORA + KV_LORA, off_kv)
    swap = jnp.arange(MLA_ROPE) ^ ROPE_FREQS
    z64 = jnp.zeros(w_in.shape[:2] + (LANES - MLA_ROPE,), w_in.dtype)
    parts = [
        seg(off_rwkv, off_rwkv + 3 * D_MODEL),
        seg(off_kv, off_kv + 2 * D_MODEL),
        seg(rw, rw + 3 * D_MODEL),
        seg(0, Q_LORA), seg(Q_LORA, Q_LORA + KV_LORA),
        k_rope, z64, k_rope[:, :, swap], z64,
        seg(rw + 3 * D_MODEL, rw + rw_in),
    ]
    w = jnp.concatenate(parts, axis=-1)
    pad = D_IN_PAD - w.shape[-1]
    return jnp.pad(w, ((0, 0), (0, 0), (0, pad))).astype(BF16)


def _layout_w_q(w_q_b):
    depth = w_q_b.shape[0]
    w = w_q_b.reshape(depth, Q_LORA, MLA_HEADS, MLA_NOPE + MLA_ROPE)
    nope, rope = w[..., :MLA_NOPE], w[..., MLA_NOPE:]
    swap = jnp.arange(MLA_ROPE) ^ ROPE_FREQS
    z64 = jnp.zeros(rope.shape[:-1] + (LANES - MLA_ROPE,), w.dtype)
    w = jnp.concatenate([nope, rope, z64, rope[..., swap], z64], axis=-1)
    return w.reshape(depth, Q_LORA, MLA_HEADS * 3 * LANES).astype(BF16)


def _rope_tables(n_ctx, n_lat):
    rows = n_lat // GRID_W
    row = jnp.repeat(jnp.arange(rows), GRID_W).astype(F32)
    colp = jnp.tile(jnp.arange(GRID_W), rows).astype(F32)
    inv_freq = ROPE_THETA ** (-jnp.arange(ROPE_FREQS, dtype=F32) / ROPE_FREQS)
    ang_lat = jnp.stack([row[:, None] * inv_freq, colp[:, None] * inv_freq], axis=1)
    ang = jnp.concatenate([jnp.zeros((n_ctx, 2, ROPE_FREQS), F32), ang_lat], axis=0)
    cos, sin = jnp.cos(ang), jnp.sin(ang)
    cos_t = jnp.stack([cos, cos], axis=2).reshape(-1, MLA_ROPE)
    sin_t = jnp.stack([-sin, sin], axis=2).reshape(-1, MLA_ROPE)
    pad = ((0, 0), (0, LANES - MLA_ROPE))
    return jnp.pad(cos_t, pad), jnp.pad(sin_t, pad)


def _block_diag2(w):
    z = jnp.zeros_like(w[0])
    return jnp.concatenate([jnp.concatenate([w[0], z], axis=1), jnp.concatenate([z, w[1]], axis=1)], axis=0)


def kernel(x, c, ctx, c_ctx, w_mod, b_mod, norm1_g, norm2_g, w_in, q_a_norm_g, w_q_b, kv_a_norm_g, w_kv_b, w_o_mla, dw_kernel, dw_bias, conv_ln_g, conv_ln_b, w_pw2, rwkv_mu, rwkv_w0, rwkv_w2, rwkv_a0, rwkv_a2, rwkv_g2, rwkv_k_k, rwkv_k_a, rwkv_r_k, rwkv_ln_g, rwkv_ln_b, w_o_rwkv, w_out, w_up, w_down, final_norm_g):
    batch, n_lat, d = x.shape
    n_ctx = ctx.shape[1]
    seq = n_ctx + n_lat
    depth = w_mod.shape[0]
    assert d == D_MODEL and batch < 8 and n_ctx % SCAN_CHUNK == 0 and seq % SCAN_CHUNK == 0
    ctx_row = batch

    xc = jnp.zeros((8, d), F32).at[:batch].set(c).at[batch].set(c_ctx)
    mod_all = _modulation(xc, w_mod, b_mod).reshape(depth, 8, 6, d).transpose(0, 2, 1, 3)

    w_in_l = _layout_w_in(w_in)
    w_q_l = _layout_w_q(w_q_b)
    w_kv_h = w_kv_b.reshape(depth, KV_LORA, MLA_HEADS, MLA_NOPE + MLA_V)
    w_k_l = w_kv_h[..., :MLA_NOPE].reshape(depth, KV_LORA, MLA_HEADS * MLA_NOPE).astype(BF16)
    w_vt_l = w_kv_h[..., MLA_NOPE:].transpose(0, 2, 3, 1).reshape(depth, MLA_HEADS * MLA_V, KV_LORA).astype(BF16)
    cos_t, sin_t = _rope_tables(n_ctx, n_lat)
    head_of = jnp.arange(D_MODEL) // RWKV_HEAD
    e = (head_of[:, None] == jnp.arange(LANES)[None, :]).astype(BF16)
    et = e.T
    rw_cols = 3 * D_MODEL
    row2 = lambda a: a.reshape(1, -1)

    z = jnp.concatenate([ctx, x], axis=1).reshape(batch * seq, d)
    for l in range(depth):
        mod = mod_all[l]
        p = _in_proj(z, mod, row2(norm1_g[l]), w_in_l[l], seq=seq, n_ctx=n_ctx, ctx_row=ctx_row)

        q, k, vt = _qkv(p, cos_t, sin_t, row2(q_a_norm_g[l]), row2(kv_a_norm_g[l]), w_q_l[l], w_k_l[l], w_vt_l[l],
                        batch=batch, seq=seq)
        att = _attention(q, k, vt, n_ctx=n_ctx).reshape(batch * seq, MLA_HEADS * MLA_V)

        cv = _conv(p, dw_kernel[l],row2(dw_bias[l]), row2(conv_ln_g[l]), row2(conv_ln_b[l]),
                   batch=batch, seq=seq, n_ctx=n_ctx)

        r_s, v_s, kk_s, kd_s, b_s, w_s, g_s, bonus = _rwkv_prep(
            p, rwkv_mu[l][:, :rw_cols], rwkv_mu[l][:, rw_cols:],
            rwkv_w0[l].reshape(1, 2 * d), rwkv_a0[l].reshape(1, 2 * d),
            _block_diag2(rwkv_w2[l]).astype(BF16), _block_diag2(rwkv_a2[l]).astype(BF16),
            rwkv_g2[l].astype(BF16), row2(rwkv_k_k[l]), row2(rwkv_k_a[l]), rwkv_r_k[l].reshape(1, d),
            e, et, batch=batch, seq=seq, n_ctx=n_ctx)
        y_fwd, y_rev = _rwkv_scan(r_s, v_s, kk_s, kd_s, b_s, w_s, batch=batch, seq=seq, n_ctx=n_ctx)

        z = _mix_out(z, mod, att, cv, y_fwd, y_rev, g_s, bonus, p,
                     w_o_mla[l].astype(BF16), w_pw2[l].astype(BF16), w_o_rwkv[l].astype(BF16),
                     w_out[l].astype(BF16), row2(rwkv_ln_g[l]), row2(rwkv_ln_b[l]), e, et,
                     seq=seq, n_ctx=n_ctx, ctx_row=ctx_row)

        z = _mlp(z, mod, row2(norm2_g[l]), w_up[l].astype(BF16), w_down[l].astype(BF16),
                 row2(final_norm_g), seq=seq, n_ctx=n_ctx, ctx_row=ctx_row, final=(l == depth - 1))

    return z.reshape(batch, seq, d)[:, n_ctx:]
```
